```python
import jax, jax.numpy as jnp
from jax import lax
import numpy as np

D_MODEL = 1024
BATCH = 2
SEQ = 8192
DEPTH = 1

SSD_EXPAND = 2
SSD_D_INNER = SSD_EXPAND * D_MODEL
SSD_HEAD_DIM = 64
SSD_N_HEADS = SSD_D_INNER // SSD_HEAD_DIM
SSD_N_GROUPS = 4
SSD_HEADS_PER_GROUP = SSD_N_HEADS // SSD_N_GROUPS
SSD_D_STATE = 128
SSD_CONV_WIDTH = 4
SSD_CHUNK = 128
SSD_CONV_DIM = SSD_D_INNER + 2 * SSD_N_GROUPS * SSD_D_STATE

ATTN_HEAD_DIM = 64
ATTN_N_HEADS = D_MODEL // ATTN_HEAD_DIM
ATTN_N_KV_HEADS = 4
ATTN_REP = ATTN_N_HEADS // ATTN_N_KV_HEADS
ATTN_WINDOW = 128
ATTN_WIDTH = ATTN_N_HEADS * ATTN_HEAD_DIM
KV_WIDTH = ATTN_N_KV_HEADS * ATTN_HEAD_DIM
ROPE_THETA = 10000.0

FFN_D_FF = 2816
FFN_CONV_WIDTH = 3

NORM_EPS = 1e-6

IN_SIZES = (SSD_D_INNER, SSD_CONV_DIM, SSD_N_HEADS, ATTN_WIDTH, KV_WIDTH, KV_WIDTH, D_MODEL, D_MODEL)
IN_PROJ_DIM = sum(IN_SIZES)

kernel_name = "hybrid_ssd_swa_sink_gated_convffn"


def _split(t, sizes):
    idx = np.cumsum(np.array(sizes))[:-1].tolist()
    return jnp.split(t, idx, axis=-1)


def rms_norm(x, w):
    xf = x.astype(jnp.float32)
    y = xf * lax.rsqrt(jnp.mean(xf * xf, axis=-1, keepdims=True) + NORM_EPS)
    return (y * w.astype(jnp.float32)).astype(x.dtype)


def gated_group_rms_norm(y, z, w):
    b, s, d = y.shape
    g = (y.astype(jnp.float32) * jax.nn.silu(z.astype(jnp.float32))).reshape(b, s, SSD_N_GROUPS, d // SSD_N_GROUPS)
    g = g * lax.rsqrt(jnp.mean(g * g, axis=-1, keepdims=True) + NORM_EPS)
    return (g.reshape(b, s, d) * w.astype(jnp.float32)).astype(y.dtype)


def causal_dwconv(x, w, bias):
    k = w.shape[0]
    y = lax.conv_general_dilated(x, w[:, None, :].astype(x.dtype), window_strides=(1,), padding=((k - 1, 0),),
                                 dimension_numbers=('NWC', 'WIO', 'NWC'), feature_group_count=x.shape[-1])
    return y + bias.astype(x.dtype)


def rope_tables(positions):
    half = ATTN_HEAD_DIM // 2
    inv_freq = ROPE_THETA ** (-jnp.arange(half, dtype=jnp.float32) * 2.0 / ATTN_HEAD_DIM)
    ang = positions.astype(jnp.float32)[..., None] * inv_freq
    return jnp.cos(ang), jnp.sin(ang)


def apply_rope(t, cos, sin):
    b, s = t.shape[:2]
    half = t.shape[-1] // 2
    shp = (b, s) + (1,) * (t.ndim - 3) + (half,)
    c, sn = cos.reshape(shp), sin.reshape(shp)
    tf = t.astype(jnp.float32)
    t1, t2 = tf[..., :half], tf[..., half:]
    return jnp.concatenate([t1 * c - t2 * sn, t2 * c + t1 * sn], axis=-1).astype(t.dtype)


def ssd_chunked(xh, dt, a, bm, cm):
    b, s, g, j, p = xh.shape
    n = bm.shape[-1]
    c = s // SSD_CHUNK
    xf = xh.astype(jnp.float32)
    X = (xf * dt[..., None]).reshape(b, c, SSD_CHUNK, g, j, p)
    adt = (dt * a).reshape(b, c, SSD_CHUNK, g, j).transpose(0, 3, 4, 1, 2)
    a_cs = jnp.cumsum(adt, axis=-1)
    Bc = bm.astype(jnp.float32).reshape(b, c, SSD_CHUNK, g, n)
    Cc = cm.astype(jnp.float32).reshape(b, c, SSD_CHUNK, g, n)
    causal = jnp.tril(jnp.ones((SSD_CHUNK, SSD_CHUNK), dtype=bool))
    seg = a_cs[..., :, None] - a_cs[..., None, :]
    lmat = jnp.exp(jnp.where(causal, seg, -jnp.inf))
    cb = jnp.einsum('bclgn,bcsgn->bcgls', Cc, Bc)
    y_diag = jnp.einsum('bcgls,bgjcls,bcsgjp->bclgjp', cb, lmat, X)
    decay_states = jnp.exp(a_cs[..., -1:] - a_cs)
    states = jnp.einsum('bclgn,bgjcl,bclgjp->bcgjpn', Bc, decay_states, X)
    chunk_decay = jnp.exp(a_cs[..., -1])

    def step(h, inp):
        st, dc = inp
        return h * dc[..., None, None] + st, h

    h0 = jnp.zeros((b, g, j, p, n), jnp.float32)
    _, prev = lax.scan(step, h0, (jnp.moveaxis(states, 1, 0), jnp.moveaxis(chunk_decay, -1, 0)))
    prev = jnp.moveaxis(prev, 0, 1)
    y_off = jnp.einsum('bclgn,bcgjpn,bgjcl->bclgjp', Cc, prev, jnp.exp(a_cs))
    return (y_diag + y_off).reshape(b, s, g, j, p)


def sliding_window_sink_attention(q, k, v, sinks):
    b, s, g, r, d = q.shape
    w = ATTN_WINDOW
    nb = s // w
    qb = q.reshape(b, nb, w, g, r, d) * (d ** -0.5)
    kb = k.reshape(b, nb, w, g, d)
    vb = v.reshape(b, nb, w, g, d)
    kk = jnp.concatenate([jnp.concatenate([jnp.zeros_like(kb[:, :1]), kb[:, :-1]], axis=1), kb], axis=2)
    vv = jnp.concatenate([jnp.concatenate([jnp.zeros_like(vb[:, :1]), vb[:, :-1]], axis=1), vb], axis=2)
    scores = jnp.einsum('bnqgrd,bnkgd->bgrnqk', qb, kk, preferred_element_type=jnp.float32)
    qpos = jnp.arange(w)[:, None] + w
    kpos = jnp.arange(2 * w)[None, :]
    diff = qpos - kpos
    band = (diff >= 0) & (diff < w)
    valid = band[None] & ((jnp.arange(nb)[:, None, None] > 0) | (kpos[None] >= w))
    scores = jnp.where(valid, scores, -jnp.inf)
    sink = sinks.astype(jnp.float32).reshape(g, r)[None, :, :, None, None, None]
    m = jnp.maximum(jnp.max(scores, axis=-1, keepdims=True), sink)
    pexp = jnp.exp(scores - m)
    probs = pexp / (jnp.sum(pexp, axis=-1, keepdims=True) + jnp.exp(sink - m))
    out = jnp.einsum('bgrnqk,bnkgd->bnqgrd', probs.astype(v.dtype), vv)
    return out.reshape(b, s, g * r * d)


def hybrid_layer(x, cos, sin, norm_mix_pre_w, w_in, ssd_conv_w, ssd_conv_b, ssd_dt_bias, ssd_a_log, ssd_d,
                 ssd_norm_w, ssd_w_out, attn_sinks, attn_w_out, w_mix_out, norm_mix_post_w,
                 norm_ffn_pre_w, ffn_w_up, ffn_conv_w, ffn_conv_b, ffn_w_down, norm_ffn_post_w):
    b, s, _ = x.shape
    u = rms_norm(x, norm_mix_pre_w)
    proj = u @ w_in
    z, xbc, dt_raw, q, k, v, gate_ssd, gate_attn = _split(proj, IN_SIZES)

    xbc = jax.nn.silu(causal_dwconv(xbc, ssd_conv_w, ssd_conv_b))
    xs, bm, cm = _split(xbc, (SSD_D_INNER, SSD_N_GROUPS * SSD_D_STATE, SSD_N_GROUPS * SSD_D_STATE))
    xh = xs.reshape(b, s, SSD_N_GROUPS, SSD_HEADS_PER_GROUP, SSD_HEAD_DIM)
    bm = bm.reshape(b, s, SSD_N_GROUPS, SSD_D_STATE)
    cm = cm.reshape(b, s, SSD_N_GROUPS, SSD_D_STATE)
    dt = jax.nn.softplus(dt_raw.astype(jnp.float32) + ssd_dt_bias.astype(jnp.float32)).reshape(b, s, SSD_N_GROUPS, SSD_HEADS_PER_GROUP)
    a = -jnp.exp(ssd_a_log.astype(jnp.float32)).reshape(SSD_N_GROUPS, SSD_HEADS_PER_GROUP)
    y = ssd_chunked(xh, dt, a, bm, cm)
    y = y + ssd_d.astype(jnp.float32).reshape(SSD_N_GROUPS, SSD_HEADS_PER_GROUP)[..., None] * xh.astype(jnp.float32)
    y = gated_group_rms_norm(y.reshape(b, s, SSD_D_INNER).astype(x.dtype), z, ssd_norm_w)
    y_ssd = y @ ssd_w_out

    q = apply_rope(q.reshape(b, s, ATTN_N_KV_HEADS, ATTN_REP, ATTN_HEAD_DIM), cos, sin)
    k = apply_rope(k.reshape(b, s, ATTN_N_KV_HEADS, ATTN_HEAD_DIM), cos, sin)
    v = v.reshape(b, s, ATTN_N_KV_HEADS, ATTN_HEAD_DIM)
    y_attn = sliding_window_sink_attention(q, k, v, attn_sinks) @ attn_w_out

    merged = jax.nn.sigmoid(gate_ssd) * y_ssd + jax.nn.sigmoid(gate_attn) * y_attn
    x = x + rms_norm(merged @ w_mix_out, norm_mix_post_w)

    h = rms_norm(x, norm_ffn_pre_w)
    up = causal_dwconv(h @ ffn_w_up, ffn_conv_w, ffn_conv_b)
    gate, val = _split(up, (FFN_D_FF, FFN_D_FF))
    ff = (jax.nn.gelu(gate, approximate=True) * val) @ ffn_w_down
    return x + rms_norm(ff, norm_ffn_post_w)


def setup_inputs(seed: int = 0) -> dict:
    key = jax.random.key(seed)
    ks = jax.random.split(key, 24)
    L = DEPTH
    f32 = jnp.float32

    def nrm(k, shape, scale):
        return jax.random.normal(k, shape, f32) * scale

    def gain(k, d):
        return 1.0 + 0.02 * jax.random.normal(k, (L, d), f32)

    x = jax.random.normal(ks[0], (BATCH, SEQ, D_MODEL), f32)
    start = jax.random.randint(ks[1], (BATCH,), 0, 4096, dtype=jnp.int32)
    positions = (start[:, None] + jnp.arange(SEQ, dtype=jnp.int32)[None, :]).astype(jnp.int32)
    u = jax.random.uniform(ks[2], (L, SSD_N_HEADS), f32)
    dt0 = jnp.exp(u * (np.log(0.1) - np.log(0.001)) + np.log(0.001)).astype(f32)
    ssd_dt_bias = dt0 + jnp.log(-jnp.expm1(-dt0))
    ssd_a_log = jnp.log(jax.random.uniform(ks[3], (L, SSD_N_HEADS), f32, minval=1.0, maxval=16.0))
    return {
        "x": x,
        "positions": positions,
        "norm_mix_pre_w": gain(ks[4], D_MODEL),
        "w_in": nrm(ks[5], (L, D_MODEL, IN_PROJ_DIM), D_MODEL ** -0.5),
        "ssd_conv_w": nrm(ks[6], (L, SSD_CONV_WIDTH, SSD_CONV_DIM), SSD_CONV_WIDTH ** -0.5),
        "ssd_conv_b": nrm(ks[7], (L, SSD_CONV_DIM), 0.02),
        "ssd_dt_bias": ssd_dt_bias,
        "ssd_a_log": ssd_a_log,
        "ssd_d": 1.0 + 0.1 * jax.random.normal(ks[8], (L, SSD_N_HEADS), f32),
        "ssd_norm_w": gain(ks[9], SSD_D_INNER),
        "ssd_w_out": nrm(ks[10], (L, SSD_D_INNER, D_MODEL), SSD_D_INNER ** -0.5),
        "attn_sinks": nrm(ks[11], (L, ATTN_N_HEADS), 1.0),
        "attn_w_out": nrm(ks[12], (L, ATTN_WIDTH, D_MODEL), ATTN_WIDTH ** -0.5),
        "w_mix_out": nrm(ks[13], (L, D_MODEL, D_MODEL), D_MODEL ** -0.5),
        "norm_mix_post_w": gain(ks[14], D_MODEL),
        "norm_ffn_pre_w": gain(ks[15], D_MODEL),
        "ffn_w_up": nrm(ks[16], (L, D_MODEL, 2 * FFN_D_FF), D_MODEL ** -0.5),
        "ffn_conv_w": nrm(ks[17], (L, FFN_CONV_WIDTH, 2 * FFN_D_FF), FFN_CONV_WIDTH ** -0.5),
        "ffn_conv_b": nrm(ks[18], (L, 2 * FFN_D_FF), 0.02),
        "ffn_w_down": nrm(ks[19], (L, FFN_D_FF, D_MODEL), FFN_D_FF ** -0.5),
        "norm_ffn_post_w": gain(ks[20], D_MODEL),
    }


def reference(x, positions, norm_mix_pre_w, w_in, ssd_conv_w, ssd_conv_b, ssd_dt_bias, ssd_a_log, ssd_d,
              ssd_norm_w, ssd_w_out, attn_sinks, attn_w_out, w_mix_out, norm_mix_post_w,
              norm_ffn_pre_w, ffn_w_up, ffn_conv_w, ffn_conv_b, ffn_w_down, norm_ffn_post_w):
    cos, sin = rope_tables(positions)
    for i in range(DEPTH):
        x = hybrid_layer(x, cos, sin, norm_mix_pre_w[i], w_in[i], ssd_conv_w[i], ssd_conv_b[i], ssd_dt_bias[i],
                         ssd_a_log[i], ssd_d[i], ssd_norm_w[i], ssd_w_out[i], attn_sinks[i], attn_w_out[i],
                         w_mix_out[i], norm_mix_post_w[i], norm_ffn_pre_w[i], ffn_w_up[i], ffn_conv_w[i],
                         ffn_conv_b[i], ffn_w_down[i], norm_ffn_post_w[i])
    return x
```

```python
import functools

import numpy as np
import jax
import jax.numpy as jnp
from jax import lax
from jax.experimental import pallas as pl
from jax.experimental.pallas import tpu as pltpu

F32 = jnp.float32
BF16 = jnp.bfloat16

D_MODEL = 1024
SSD_D_INNER = 2048
SSD_HEAD_DIM = 64
SSD_N_HEADS = 32
SSD_N_GROUPS = 4
SSD_HEADS_PER_GROUP = 8
SSD_D_STATE = 128
SSD_CONV_WIDTH = 4
SSD_CONV_DIM = 3072
SSD_GROUP_WIDTH = SSD_D_INNER // SSD_N_GROUPS
CHUNK = 128

ATTN_HEAD_DIM = 64
ATTN_N_HEADS = 16
ATTN_N_KV_HEADS = 4
ATTN_REP = 4
ATTN_WIDTH = 1024
KV_WIDTH = 256
QKV_WIDTH = ATTN_WIDTH + 2 * KV_WIDTH
ROPE_THETA = 10000.0

FFN_D_FF = 2816
FFN_CONV_WIDTH = 3
NORM_EPS = 1e-6

LANES = 128
SUBLANES = 8
DT_PAD = LANES
GATES_WIDTH = 2 * D_MODEL

_COL_Z = 0
_COL_XBC = _COL_Z + SSD_D_INNER
_COL_QKV = _COL_XBC + SSD_CONV_DIM
_COL_GATES = _COL_QKV + QKV_WIDTH
_COL_DT = _COL_GATES + GATES_WIDTH
IN_PROJ_PAD = _COL_DT + DT_PAD

VMEM_LIMIT = 56 * 1024 * 1024


def _resident(shape):
    nd = len(shape)
    return pl.BlockSpec(shape, lambda *_: (0,) * nd, pipeline_mode=pl.Buffered(1))


def _sigmoid(v):
    return 1.0 / (1.0 + jnp.exp(-v))


def _split3(v):
    hi = v.astype(BF16).astype(F32)
    r = v - hi
    mid = r.astype(BF16).astype(F32)
    lo = (r - mid).astype(BF16).astype(F32)
    return hi, mid, lo


IN_PROJ_NCHUNK = 512


def _in_proj_kernel(x_ref, nw_ref, w_ref, z_ref, xbc_ref, qkv_ref, gates_ref, dt_ref):
    x = x_ref[...]
    ms = jnp.mean(x * x, axis=-1, keepdims=True)
    u = (x * lax.rsqrt(ms + NORM_EPS) * nw_ref[...]).astype(BF16)
    for out_ref, col0, width in ((z_ref, _COL_Z, SSD_D_INNER), (xbc_ref, _COL_XBC, SSD_CONV_DIM),
                                 (qkv_ref, _COL_QKV, QKV_WIDTH), (gates_ref, _COL_GATES, GATES_WIDTH),
                                 (dt_ref, _COL_DT, DT_PAD)):
        step = min(IN_PROJ_NCHUNK, width)
        for c in range(0, width, step):
            r = jnp.dot(u, w_ref[:, col0 + c:col0 + c + step], preferred_element_type=F32)
            out_ref[:, c:c + step] = r.astype(out_ref.dtype)


def _in_proj(x2, nw, w_all, tm):
    t = x2.shape[0]
    row = lambda w: pl.BlockSpec((tm, w), lambda i: (i, 0))
    return pl.pallas_call(
        _in_proj_kernel,
        grid=(t // tm,),
        in_specs=[row(D_MODEL), _resident((1, D_MODEL)), _resident((D_MODEL, IN_PROJ_PAD))],
        out_specs=[row(SSD_D_INNER), row(SSD_CONV_DIM), row(QKV_WIDTH), row(GATES_WIDTH), row(DT_PAD)],
        out_shape=[jax.ShapeDtypeStruct((t, SSD_D_INNER), BF16), jax.ShapeDtypeStruct((t, SSD_CONV_DIM), BF16),
                   jax.ShapeDtypeStruct((t, QKV_WIDTH), BF16), jax.ShapeDtypeStruct((t, GATES_WIDTH), BF16),
                   jax.ShapeDtypeStruct((t, DT_PAD), F32)],
        compiler_params=pltpu.CompilerParams(dimension_semantics=("arbitrary",), vmem_limit_bytes=VMEM_LIMIT),
        name="in_proj",
    )(x2, nw, w_all)


CONV_HALO = SUBLANES


def _ssd_kernel(z_ref, xbc_ref, dt_ref, cw_ref, cb_ref, dtb_ref, alog_ref, dexp_ref, nw_ref, e3_ref,
                yn_ref, cbuf, state, xs, yacc):
    L = CHUNK

    @pl.when(pl.program_id(1) == 0)
    def _():
        cbuf[0:CONV_HALO, :] = jnp.zeros((CONV_HALO, SSD_CONV_DIM), F32)
        state[...] = jnp.zeros(state.shape, F32)

    cbuf[CONV_HALO:CONV_HALO + L, :] = xbc_ref[...].astype(F32)
    bmat, cmat = [], []
    for c0 in range(0, SSD_CONV_DIM, SSD_GROUP_WIDTH):
        acc = cb_ref[:, c0:c0 + SSD_GROUP_WIDTH]
        for k in range(SSD_CONV_WIDTH):
            r0 = CONV_HALO - (SSD_CONV_WIDTH - 1) + k
            acc = acc + cw_ref[k:k + 1, c0:c0 + SSD_GROUP_WIDTH] * cbuf[r0:r0 + L, c0:c0 + SSD_GROUP_WIDTH]
        act = acc * _sigmoid(acc)
        if c0 < SSD_D_INNER:
            xs[:, c0:c0 + SSD_GROUP_WIDTH] = act
        else:
            for g0 in range(0, SSD_GROUP_WIDTH, SSD_D_STATE):
                (bmat if c0 + g0 < SSD_D_INNER + SSD_N_GROUPS * SSD_D_STATE else cmat).append(
                    act[:, g0:g0 + SSD_D_STATE].astype(BF16))
    cbuf[0:CONV_HALO, :] = cbuf[L:L + CONV_HALO, :]

    lane = lax.broadcasted_iota(jnp.int32, (L, LANES), 1)
    head_lane = lane < SSD_N_HEADS
    dtr = dt_ref[...] + dtb_ref[...]
    dt = jnp.maximum(dtr, 0.0) + jnp.log(1.0 + jnp.exp(-jnp.abs(dtr)))
    dt = jnp.where(head_lane, dt, 0.0)
    adt = dt * (-jnp.exp(alog_ref[...]))
    row_i = lax.broadcasted_iota(jnp.int32, (L, L), 0)
    col_i = lax.broadcasted_iota(jnp.int32, (L, L), 1)
    causal = col_i <= row_i
    tril = jnp.where(causal, 1.0, 0.0).astype(BF16)
    acs = sum(jnp.dot(tril, p.astype(BF16), preferred_element_type=F32) for p in _split3(adt))
    acs_t = acs.T
    dt_t = dt.T

    def pack3(v):
        hi, mid, lo = _split3(v)
        return jnp.where(head_lane, hi, jnp.where(lane < 2 * SSD_N_HEADS, pltpu.roll(mid, SSD_N_HEADS, 1),
                                                  pltpu.roll(lo, 2 * SSD_N_HEADS, 1))).astype(BF16)

    exp_in = jnp.concatenate([pack3(acs), pack3(dt)], axis=0)
    expd = jnp.dot(exp_in, e3_ref[...], preferred_element_type=F32)
    acs_x = expd[0:L, :]
    dt_x = expd[L:2 * L, :]
    acs_last = acs_x[L - 1:L, :]
    e_x = jnp.exp(acs_x)
    w_x = dt_x * jnp.exp(acs_last - acs_x)
    cd_x = jnp.exp(acs_last)

    for g in range(SSD_N_GROUPS):
        gs = slice(g * SSD_GROUP_WIDTH, (g + 1) * SSD_GROUP_WIDTH)
        xg = xs[:, gs]
        st = state[g]
        yoff = jnp.dot(cmat[g], st.astype(BF16), preferred_element_type=F32) * e_x[:, gs]
        xd = (xg * w_x[:, gs]).astype(BF16)
        upd = lax.dot_general(bmat[g], xd, (((0,), (0,)), ((), ())), preferred_element_type=F32)
        state[g] = st * cd_x[:, gs] + upd
        yacc[:, gs] = yoff + dexp_ref[:, gs] * xg
        cbm = lax.dot_general(cmat[g], bmat[g], (((1,), (1,)), ((), ())), preferred_element_type=F32)
        xb = xg.astype(BF16)
        for jp in range(0, SSD_HEADS_PER_GROUP, 2):
            pair = []
            for j in (jp, jp + 1):
                h = g * SSD_HEADS_PER_GROUP + j
                seg = acs[:, h:h + 1] - acs_t[h:h + 1, :]
                m = jnp.where(causal, jnp.exp(seg), 0.0) * cbm * dt_t[h:h + 1, :]
                pair.append(jnp.dot(m.astype(BF16), xb[:, j * SSD_HEAD_DIM:(j + 1) * SSD_HEAD_DIM],
                                    preferred_element_type=F32))
            c0 = g * SSD_GROUP_WIDTH + jp * SSD_HEAD_DIM
            yacc[:, c0:c0 + LANES] += jnp.concatenate(pair, axis=1)

    for g in range(SSD_N_GROUPS):
        gs = slice(g * SSD_GROUP_WIDTH, (g + 1) * SSD_GROUP_WIDTH)
        zg = z_ref[:, gs].astype(F32)
        gv = yacc[:, gs] * (zg * _sigmoid(zg))
        ms = jnp.mean(gv * gv, axis=-1, keepdims=True)
        yn_ref[:, gs] = (gv * lax.rsqrt(ms + NORM_EPS) * nw_ref[:, gs]).astype(BF16)


def _ssd(z, xbc, dtp, cw, cb, dtb, alog, dexp, nw, e3, batch):
    t = z.shape[0]
    nc = t // batch // CHUNK
    row = lambda w: pl.BlockSpec((CHUNK, w), lambda b, c: (b * nc + c, 0))
    return pl.pallas_call(
        _ssd_kernel,
        grid=(batch, nc),
        in_specs=[row(SSD_D_INNER), row(SSD_CONV_DIM), row(DT_PAD),
                  _resident((SSD_CONV_WIDTH, SSD_CONV_DIM)), _resident((1, SSD_CONV_DIM)),
                  _resident((1, DT_PAD)), _resident((1, DT_PAD)), _resident((1, SSD_D_INNER)),
                  _resident((1, SSD_D_INNER)), _resident((LANES, SSD_D_INNER))],
        out_specs=row(SSD_D_INNER),
        out_shape=jax.ShapeDtypeStruct((t, SSD_D_INNER), BF16),
        scratch_shapes=[pltpu.VMEM((CHUNK + CONV_HALO, SSD_CONV_DIM), F32),
                        pltpu.VMEM((SSD_N_GROUPS, SSD_D_STATE, SSD_GROUP_WIDTH), F32),
                        pltpu.VMEM((CHUNK, SSD_D_INNER), F32),
                        pltpu.VMEM((CHUNK, SSD_D_INNER), F32)],
        compiler_params=pltpu.CompilerParams(dimension_semantics=("arbitrary", "arbitrary"),
                                             vmem_limit_bytes=VMEM_LIMIT),
        name="ssd",
    )(z, xbc, dtp, cw, cb, dtb, alog, dexp, nw, e3)


def _attn_kernel(sink_ref, qkv_ref, pos_ref, invf_ref, ao_ref, q_s, kk, vv):
    L = CHUNK
    blk = pl.program_id(1)

    @pl.when(blk == 0)
    def _():
        kk[L:2 * L, :] = jnp.zeros((L, KV_WIDTH), BF16)
        vv[L:2 * L, :] = jnp.zeros((L, KV_WIDTH), BF16)

    @pl.when(blk > 0)
    def _():
        kk[L:2 * L, :] = kk[0:L, :]
        vv[L:2 * L, :] = vv[0:L, :]

    lane = lax.broadcasted_iota(jnp.int32, (L, LANES), 1)
    first_half = (lane % ATTN_HEAD_DIM) < (ATTN_HEAD_DIM // 2)
    ang = pos_ref[...].astype(F32) * invf_ref[...]
    cosv = jnp.cos(ang)
    sinv = jnp.where(first_half, -jnp.sin(ang), jnp.sin(ang))

    def rope(tile):
        swapped = jnp.where(first_half, pltpu.roll(tile, LANES - ATTN_HEAD_DIM // 2, 1),
                            pltpu.roll(tile, ATTN_HEAD_DIM // 2, 1))
        return tile * cosv + swapped * sinv

    scale = ATTN_HEAD_DIM ** -0.5
    for c in range(0, ATTN_WIDTH, LANES):
        q_s[:, c:c + LANES] = (rope(qkv_ref[:, c:c + LANES].astype(F32)) * scale).astype(BF16)
    for c in range(0, KV_WIDTH, LANES):
        kk[0:L, c:c + LANES] = rope(qkv_ref[:, ATTN_WIDTH + c:ATTN_WIDTH + c + LANES].astype(F32)).astype(BF16)
    vv[0:L, :] = qkv_ref[:, ATTN_WIDTH + KV_WIDTH:QKV_WIDTH]

    rows = ATTN_REP * L
    row_i = lax.broadcasted_iota(jnp.int32, (rows, L), 0)
    col_i = lax.broadcasted_iota(jnp.int32, (rows, L), 1)
    in_cur = col_i <= (row_i % L)
    has_prev = blk > 0
    nt = (((1,), (1,)), ((), ()))
    for g in range(ATTN_N_KV_HEADS):
        ks = slice(g * ATTN_HEAD_DIM, (g + 1) * ATTN_HEAD_DIM)
        heads = range(g * ATTN_REP, (g + 1) * ATTN_REP)
        qg = jnp.concatenate([q_s[:, h * ATTN_HEAD_DIM:(h + 1) * ATTN_HEAD_DIM] for h in heads], axis=0)
        s2 = lax.dot_general(qg, kk[:, ks], nt, preferred_element_type=F32)
        s = jnp.where(in_cur, s2[:, 0:L], jnp.where(has_prev, s2[:, L:2 * L], -jnp.inf))
        sink = jnp.concatenate([jnp.full((L, 1), sink_ref[h], F32) for h in heads], axis=0)
        m = jnp.maximum(jnp.max(s, axis=-1, keepdims=True), sink)
        p = jnp.exp(s - m)
        denom = jnp.sum(p, axis=-1, keepdims=True) + jnp.exp(sink - m)
        pv = jnp.concatenate([jnp.where(in_cur, p, 0.0), jnp.where(in_cur, 0.0, p)], axis=1).astype(BF16)
        o = jnp.dot(pv, vv[:, ks], preferred_element_type=F32) / denom
        for r in range(0, ATTN_REP, 2):
            c0 = (g * ATTN_REP + r) * ATTN_HEAD_DIM
            ao_ref[:, c0:c0 + LANES] = jnp.concatenate([o[r * L:(r + 1) * L], o[(r + 1) * L:(r + 2) * L]],
                                                       axis=1).astype(BF16)


def _attn(sinks, qkv, pos2, invf, batch):
    t = qkv.shape[0]
    nb = t // batch // CHUNK
    row = lambda w: pl.BlockSpec((CHUNK, w), lambda b, c: (b * nb + c, 0))
    return pl.pallas_call(
        _attn_kernel,
        grid=(batch, nb),
        in_specs=[pl.BlockSpec(memory_space=pltpu.SMEM), row(QKV_WIDTH), row(1), _resident((1, LANES))],
        out_specs=row(ATTN_WIDTH),
        out_shape=jax.ShapeDtypeStruct((t, ATTN_WIDTH), BF16),
        scratch_shapes=[pltpu.VMEM((CHUNK, ATTN_WIDTH), BF16), pltpu.VMEM((2 * CHUNK, KV_WIDTH), BF16),
                        pltpu.VMEM((2 * CHUNK, KV_WIDTH), BF16)],
        compiler_params=pltpu.CompilerParams(dimension_semantics=("arbitrary", "arbitrary"),
                                             vmem_limit_bytes=VMEM_LIMIT),
        name="attn",
    )(sinks, qkv, pos2, invf)


def _mix_out_kernel(yn_ref, ao_ref, gates_ref, x_ref, wso_ref, wao_ref, wmix_ref, nw_ref, o_ref):
    ys = jnp.dot(yn_ref[...], wso_ref[...], preferred_element_type=F32)
    ya = jnp.dot(ao_ref[...], wao_ref[...], preferred_element_type=F32)
    gs = _sigmoid(gates_ref[:, 0:D_MODEL].astype(F32))
    ga = _sigmoid(gates_ref[:, D_MODEL:GATES_WIDTH].astype(F32))
    merged = (gs * ys + ga * ya).astype(BF16)
    mo = jnp.dot(merged, wmix_ref[...], preferred_element_type=F32)
    ms = jnp.mean(mo * mo, axis=-1, keepdims=True)
    o_ref[...] = x_ref[...] + mo * lax.rsqrt(ms + NORM_EPS) * nw_ref[...]


def _mix_out(yn, ao, gates, x2, wso, wao, wmix, nw, tm):
    t = x2.shape[0]
    row = lambda w: pl.BlockSpec((tm, w), lambda i: (i, 0))
    return pl.pallas_call(
        _mix_out_kernel,
        grid=(t // tm,),
        in_specs=[row(SSD_D_INNER), row(ATTN_WIDTH), row(GATES_WIDTH), row(D_MODEL),
                  _resident((SSD_D_INNER, D_MODEL)), _resident((ATTN_WIDTH, D_MODEL)),
                  _resident((D_MODEL, D_MODEL)), _resident((1, D_MODEL))],
        out_specs=row(D_MODEL),
        out_shape=jax.ShapeDtypeStruct((t, D_MODEL), F32),
        compiler_params=pltpu.CompilerParams(dimension_semantics=("arbitrary",), vmem_limit_bytes=VMEM_LIMIT),
        name="mix_out",
    )(yn, ao, gates, x2, wso, wao, wmix, nw)


FFN_NCHUNK = 256
FFN_HALO = SUBLANES
_GELU_C = float(np.sqrt(2.0 / np.pi))


def _gelu_tanh(v):
    return 0.5 * v * (1.0 + jnp.tanh(_GELU_C * (v + 0.044715 * (v * v * v))))


def _ffn_kernel(x_ref, npre_ref, wup_ref, cw_ref, cb_ref, wdn_ref, npost_ref, o_ref, carry, ubuf, acc,
                *, tiles_per_seq):
    tm = x_ref.shape[0]

    @pl.when(pl.program_id(0) % tiles_per_seq == 0)
    def _():
        carry[...] = jnp.zeros(carry.shape, F32)

    x = x_ref[...]
    ms = jnp.mean(x * x, axis=-1, keepdims=True)
    h = (x * lax.rsqrt(ms + NORM_EPS) * npre_ref[...]).astype(BF16)

    def conv_chunk(c0):
        cs = slice(c0, c0 + FFN_NCHUNK)
        ubuf[0:FFN_HALO, :] = carry[:, cs]
        ubuf[FFN_HALO:FFN_HALO + tm, :] = jnp.dot(h, wup_ref[:, cs], preferred_element_type=F32)
        carry[:, cs] = ubuf[tm:tm + FFN_HALO, :]
        out = cb_ref[:, cs]
        for k in range(FFN_CONV_WIDTH):
            r0 = FFN_HALO - (FFN_CONV_WIDTH - 1) + k
            out = out + cw_ref[k:k + 1, cs] * ubuf[r0:r0 + tm, :]
        return out

    for i, c0 in enumerate(range(0, FFN_D_FF, FFN_NCHUNK)):
        gate = conv_chunk(c0)
        val = conv_chunk(FFN_D_FF + c0)
        act = (_gelu_tanh(gate) * val).astype(BF16)
        part = jnp.dot(act, wdn_ref[c0:c0 + FFN_NCHUNK, :], preferred_element_type=F32)
        if i == 0:
            acc[...] = part
        else:
            acc[...] += part
    ff = acc[...]
    ms2 = jnp.mean(ff * ff, axis=-1, keepdims=True)
    o_ref[...] = x + ff * lax.rsqrt(ms2 + NORM_EPS) * npost_ref[...]


def _ffn(x1, npre, wup, cw, cb, wdn, npost, tm, batch):
    t = x1.shape[0]
    row = lambda w: pl.BlockSpec((tm, w), lambda i: (i, 0))
    return pl.pallas_call(
        functools.partial(_ffn_kernel, tiles_per_seq=t // batch // tm),
        grid=(t // tm,),
        in_specs=[row(D_MODEL), _resident((1, D_MODEL)), _resident((D_MODEL, 2 * FFN_D_FF)),
                  _resident((FFN_CONV_WIDTH, 2 * FFN_D_FF)), _resident((1, 2 * FFN_D_FF)),
                  _resident((FFN_D_FF, D_MODEL)), _resident((1, D_MODEL))],
        out_specs=row(D_MODEL),
        out_shape=jax.ShapeDtypeStruct((t, D_MODEL), F32),
        scratch_shapes=[pltpu.VMEM((FFN_HALO, 2 * FFN_D_FF), F32),
                        pltpu.VMEM((tm + FFN_HALO, FFN_NCHUNK), F32),
                        pltpu.VMEM((tm, D_MODEL), F32)],
        compiler_params=pltpu.CompilerParams(dimension_semantics=("arbitrary",), vmem_limit_bytes=VMEM_LIMIT),
        name="ffn",
    )(x1, npre, wup, cw, cb, wdn, npost)


def _expansion_matrix():
    e = np.zeros((LANES, SSD_D_INNER), np.float32)
    ch = np.arange(SSD_D_INNER)
    for part in range(3):
        e[part * SSD_N_HEADS + ch // SSD_HEAD_DIM, ch] = 1.0
    return jnp.asarray(e, dtype=BF16)


def _rope_inv_freq():
    half = ATTN_HEAD_DIM // 2
    inv = ROPE_THETA ** (-jnp.arange(half, dtype=F32) * 2.0 / ATTN_HEAD_DIM)
    return jnp.tile(inv, LANES // half).reshape(1, LANES)


def _layer(x2, pos2, batch, norm_mix_pre_w, w_in, ssd_conv_w, ssd_conv_b, ssd_dt_bias, ssd_a_log, ssd_d,
           ssd_norm_w, ssd_w_out, attn_sinks, attn_w_out, w_mix_out, norm_mix_post_w, norm_ffn_pre_w,
           ffn_w_up, ffn_conv_w, ffn_conv_b, ffn_w_down, norm_ffn_post_w, tm):
    o = np.cumsum((0, SSD_D_INNER, SSD_CONV_DIM, SSD_N_HEADS, ATTN_WIDTH, KV_WIDTH, KV_WIDTH, D_MODEL, D_MODEL))
    w_all = jnp.concatenate([w_in[:, o[0]:o[2]], w_in[:, o[3]:o[8]], w_in[:, o[2]:o[3]],
                             jnp.zeros((D_MODEL, DT_PAD - SSD_N_HEADS), w_in.dtype)], axis=1).astype(BF16)
    row = lambda v: v.reshape(1, -1).astype(F32)
    pad_heads = lambda v: jnp.pad(v.astype(F32), (0, DT_PAD - SSD_N_HEADS)).reshape(1, DT_PAD)

    z, xbc, qkv, gates, dtp = _in_proj(x2, row(norm_mix_pre_w), w_all, tm)
    yn = _ssd(z, xbc, dtp, ssd_conv_w.astype(F32), row(ssd_conv_b), pad_heads(ssd_dt_bias), pad_heads(ssd_a_log),
              row(jnp.repeat(ssd_d, SSD_HEAD_DIM)), row(ssd_norm_w), _expansion_matrix(), batch)
    ao = _attn(attn_sinks.astype(F32), qkv, pos2, _rope_inv_freq(), batch)
    x1 = _mix_out(yn, ao, gates, x2, ssd_w_out.astype(BF16), attn_w_out.astype(BF16), w_mix_out.astype(BF16),
                  row(norm_mix_post_w), tm)
    return _ffn(x1, row(norm_ffn_pre_w), ffn_w_up.astype(BF16), ffn_conv_w.astype(F32), row(ffn_conv_b),
                ffn_w_down.astype(BF16), row(norm_ffn_post_w), tm, batch)


def kernel(x, positions, norm_mix_pre_w, w_in, ssd_conv_w, ssd_conv_b, ssd_dt_bias, ssd_a_log, ssd_d, ssd_norm_w,
           ssd_w_out, attn_sinks, attn_w_out, w_mix_out, norm_mix_post_w, norm_ffn_pre_w, ffn_w_up, ffn_conv_w,
           ffn_conv_b, ffn_w_down, norm_ffn_post_w):
    batch, seq, d = x.shape
    assert d == D_MODEL and seq % CHUNK == 0
    tm = 512 if seq % 512 == 0 else CHUNK
    x2 = x.reshape(batch * seq, d)
    pos2 = positions.reshape(batch * seq, 1)
    for i in range(w_in.shape[0]):
        x2 = _layer(x2, pos2, batch, norm_mix_pre_w[i], w_in[i], ssd_conv_w[i], ssd_conv_b[i], ssd_dt_bias[i],
                    ssd_a_log[i], ssd_d[i], ssd_norm_w[i], ssd_w_out[i], attn_sinks[i], attn_w_out[i],
                    w_mix_out[i], norm_mix_post_w[i], norm_ffn_pre_w[i], ffn_w_up[i], ffn_conv_w[i],
                    ffn_conv_b[i], ffn_w_down[i], norm_ffn_post_w[i], tm)
    return x2.reshape(batch, seq, d)
```

```python
import functools

import numpy as np
import jax
import jax.numpy as jnp
from jax import lax
from jax.experimental import pallas as pl
from jax.experimental.pallas import tpu as pltpu

F32 = jnp.float32
BF16 = jnp.bfloat16

D_MODEL = 1024
SSD_D_INNER = 2048
SSD_HEAD_DIM = 64
SSD_N_HEADS = 32
SSD_N_GROUPS = 4
SSD_HEADS_PER_GROUP = 8
SSD_D_STATE = 128
SSD_CONV_WIDTH = 4
SSD_CONV_DIM = 3072
SSD_GROUP_WIDTH = SSD_D_INNER // SSD_N_GROUPS
CHUNK = 128

ATTN_HEAD_DIM = 64
ATTN_N_HEADS = 16
ATTN_N_KV_HEADS = 4
ATTN_REP = 4
ATTN_WIDTH = 1024
KV_WIDTH = 256
QKV_WIDTH = ATTN_WIDTH + 2 * KV_WIDTH
ROPE_THETA = 10000.0

FFN_D_FF = 2816
FFN_CONV_WIDTH = 3
NORM_EPS = 1e-6

LANES = 128
SUBLANES = 8
DT_PAD = LANES
GATES_WIDTH = 2 * D_MODEL

_COL_Z = 0
_COL_XBC = _COL_Z + SSD_D_INNER
_COL_GATES = _COL_XBC + SSD_CONV_DIM
_COL_DT = _COL_GATES + GATES_WIDTH
IN_PROJ_PAD = _COL_DT + DT_PAD

VMEM_LIMIT = 56 * 1024 * 1024


def _resident(shape):
    nd = len(shape)
    return pl.BlockSpec(shape, lambda *_: (0,) * nd, pipeline_mode=pl.Buffered(1))


def _sigmoid(v):
    return 1.0 / (1.0 + jnp.exp(-v))


def _split3(v):
    hi = v.astype(BF16).astype(F32)
    r = v - hi
    mid = r.astype(BF16).astype(F32)
    lo = (r - mid).astype(BF16).astype(F32)
    return hi, mid, lo


IN_PROJ_NCHUNK = 512
IN_PROJ_CONV_CHUNK = 256


IN_PROJ_TCHUNK = 256
_NT = (((1,), (1,)), ((), ()))
_TN = (((0,), (0,)), ((), ()))


CONV_HALO = SUBLANES
_LOG2E = float(np.log2(np.e))


def _silu(v):
    return v / (1.0 + jnp.exp2(v * -_LOG2E))


def _causal_conv_rows(r, halo, w, b):
    taps, (rows, width) = w.shape[0], r.shape
    nblk = rows // SUBLANES
    full = jnp.concatenate([halo, r], axis=0).reshape(nblk + 1, SUBLANES, width)
    sub = lax.broadcasted_iota(jnp.int32, (nblk, SUBLANES, width), 1)
    out = b + w[taps - 1:taps, :] * r
    for s in range(1, taps):
        merged = jnp.where(sub >= SUBLANES - s, full[0:nblk], full[1:nblk + 1])
        shifted = pltpu.roll(merged, s, 1).reshape(rows, width)
        out = out + w[taps - 1 - s:taps - s, :] * shifted
    return out


def _in_proj_kernel(x_ref, nw_ref, w_ref, wt_ref, cw_ref, cb_ref, z_ref, xbc_ref, gates_ref, dt_ref, qkvt_ref,
                    carry, *, tiles_per_seq):
    tm = x_ref.shape[0]

    @pl.when(pl.program_id(0) % tiles_per_seq == 0)
    def _():
        carry[...] = jnp.zeros(carry.shape, F32)

    x = x_ref[...]
    ms = jnp.mean(x * x, axis=-1, keepdims=True)
    u = (x * lax.rsqrt(ms + NORM_EPS) * nw_ref[...]).astype(BF16)
    def plain(out_ref, col0, c, step):
        r = jnp.dot(u, w_ref[:, col0 + c:col0 + c + step], preferred_element_type=F32)
        out_ref[:, c:c + step] = r.astype(out_ref.dtype)

    def transposed(r0):
        r = lax.dot_general(wt_ref[r0:r0 + IN_PROJ_TCHUNK, :], u, _NT, preferred_element_type=F32)
        qkvt_ref[r0:r0 + IN_PROJ_TCHUNK, :] = r.astype(BF16)

    fillers = [functools.partial(plain, out_ref, col0, c, min(IN_PROJ_NCHUNK, width))
               for out_ref, col0, width in ((z_ref, _COL_Z, SSD_D_INNER), (gates_ref, _COL_GATES, GATES_WIDTH),
                                            (dt_ref, _COL_DT, DT_PAD))
               for c in range(0, width, min(IN_PROJ_NCHUNK, width))]
    fillers += [functools.partial(transposed, r0) for r0 in range(0, QKV_WIDTH, IN_PROJ_TCHUNK)]
    n_conv = SSD_CONV_DIM // IN_PROJ_CONV_CHUNK
    per_conv = -(-len(fillers) // n_conv)

    for i in range(n_conv):
        c = i * IN_PROJ_CONV_CHUNK
        cs = slice(c, c + IN_PROJ_CONV_CHUNK)
        r = jnp.dot(u, w_ref[:, _COL_XBC + c:_COL_XBC + c + IN_PROJ_CONV_CHUNK], preferred_element_type=F32)
        for f in fillers[i * per_conv:(i + 1) * per_conv]:
            f()
        acc = _causal_conv_rows(r, carry[:, cs], cw_ref[:, cs], cb_ref[:, cs])
        carry[:, cs] = r[tm - CONV_HALO:tm, :]
        xbc_ref[:, cs] = _silu(acc).astype(BF16)


def _in_proj(x2, nw, w_all, w_t, cw, cb, tm, batch):
    t = x2.shape[0]
    row = lambda w: pl.BlockSpec((tm, w), lambda i: (i, 0))
    return pl.pallas_call(
        functools.partial(_in_proj_kernel, tiles_per_seq=t // batch // tm),
        grid=(t // tm,),
        in_specs=[row(D_MODEL), _resident((1, D_MODEL)), _resident((D_MODEL, IN_PROJ_PAD)),
                  _resident((QKV_WIDTH, D_MODEL)), _resident((SSD_CONV_WIDTH, SSD_CONV_DIM)),
                  _resident((1, SSD_CONV_DIM))],
        out_specs=[row(SSD_D_INNER), row(SSD_CONV_DIM), row(GATES_WIDTH), row(DT_PAD),
                   pl.BlockSpec((QKV_WIDTH, tm), lambda i: (0, i))],
        out_shape=[jax.ShapeDtypeStruct((t, SSD_D_INNER), BF16), jax.ShapeDtypeStruct((t, SSD_CONV_DIM), BF16),
                   jax.ShapeDtypeStruct((t, GATES_WIDTH), BF16), jax.ShapeDtypeStruct((t, DT_PAD), F32),
                   jax.ShapeDtypeStruct((QKV_WIDTH, t), BF16)],
        scratch_shapes=[pltpu.VMEM((CONV_HALO, SSD_CONV_DIM), F32)],
        compiler_params=pltpu.CompilerParams(dimension_semantics=("arbitrary",), vmem_limit_bytes=VMEM_LIMIT),
        name="in_proj",
    )(x2, nw, w_all, w_t, cw, cb)


_COL_B = SSD_D_INNER
_COL_C = SSD_D_INNER + SSD_N_GROUPS * SSD_D_STATE


def _ssd_kernel(z_ref, xbc_ref, dt_ref, dtb_ref, alog_ref, dexp_ref, nw_ref, e3_ref, yn_ref, state, yacc):
    L = CHUNK

    @pl.when(pl.program_id(1) == 0)
    def _():
        state[...] = jnp.zeros(state.shape, F32)

    lane = lax.broadcasted_iota(jnp.int32, (L, LANES), 1)
    head_lane = lane < SSD_N_HEADS
    dtr = dt_ref[...] + dtb_ref[...]
    dt = jnp.maximum(dtr, 0.0) + jnp.log(1.0 + jnp.exp(-jnp.abs(dtr)))
    dt = jnp.where(head_lane, dt, 0.0)
    adt = dt * (-jnp.exp(alog_ref[...]))
    row_i = lax.broadcasted_iota(jnp.int32, (L, L), 0)
    col_i = lax.broadcasted_iota(jnp.int32, (L, L), 1)
    causal = col_i <= row_i
    tril = jnp.where(causal, 1.0, 0.0).astype(BF16)
    acs = sum(jnp.dot(tril, p.astype(BF16), preferred_element_type=F32) for p in _split3(adt)) * _LOG2E
    acs_t = acs.T
    dt_t = dt.T

    def pack3(v):
        hi, mid, lo = _split3(v)
        return jnp.where(head_lane, hi, jnp.where(lane < 2 * SSD_N_HEADS, pltpu.roll(mid, SSD_N_HEADS, 1),
                                                  pltpu.roll(lo, 2 * SSD_N_HEADS, 1))).astype(BF16)

    exp_in = jnp.concatenate([pack3(acs), pack3(dt)], axis=0)
    expd = jnp.dot(exp_in, e3_ref[...], preferred_element_type=F32)
    acs_x = expd[0:L, :]
    dt_x = expd[L:2 * L, :]
    acs_last = acs_x[L - 1:L, :]
    e_x = jnp.exp2(acs_x)
    w_x = dt_x * jnp.exp2(acs_last - acs_x)
    cd_x = jnp.exp2(acs_last)

    for g in range(SSD_N_GROUPS):
        gs = slice(g * SSD_GROUP_WIDTH, (g + 1) * SSD_GROUP_WIDTH)
        xb = xbc_ref[:, gs]
        xg = xb.astype(F32)
        bm = xbc_ref[:, _COL_B + g * SSD_D_STATE:_COL_B + (g + 1) * SSD_D_STATE]
        cm = xbc_ref[:, _COL_C + g * SSD_D_STATE:_COL_C + (g + 1) * SSD_D_STATE]
        st = state[g]
        yoff = jnp.dot(cm, st.astype(BF16), preferred_element_type=F32) * e_x[:, gs]
        xd = (xg * w_x[:, gs]).astype(BF16)
        upd = lax.dot_general(bm, xd, _TN, preferred_element_type=F32)
        state[g] = st * cd_x[:, gs] + upd
        yacc[:, gs] = yoff + dexp_ref[:, gs] * xg
        cbm = lax.dot_general(cm, bm, _NT, preferred_element_type=F32)
        for jp in range(0, SSD_HEADS_PER_GROUP, 2):
            pair = []
            for j in (jp, jp + 1):
                h = g * SSD_HEADS_PER_GROUP + j
                seg = acs[:, h:h + 1] - acs_t[h:h + 1, :]
                m = jnp.where(causal, jnp.exp2(seg), 0.0) * cbm * dt_t[h:h + 1, :]
                pair.append(jnp.dot(m.astype(BF16), xb[:, j * SSD_HEAD_DIM:(j + 1) * SSD_HEAD_DIM],
                                    preferred_element_type=F32))
            c0 = g * SSD_GROUP_WIDTH + jp * SSD_HEAD_DIM
            yacc[:, c0:c0 + LANES] += jnp.concatenate(pair, axis=1)

    for g in range(SSD_N_GROUPS):
        gs = slice(g * SSD_GROUP_WIDTH, (g + 1) * SSD_GROUP_WIDTH)
        zg = z_ref[:, gs].astype(F32)
        gv = yacc[:, gs] * _silu(zg)
        ms = jnp.mean(gv * gv, axis=-1, keepdims=True)
        yn_ref[:, gs] = (gv * lax.rsqrt(ms + NORM_EPS) * nw_ref[:, gs]).astype(BF16)


def _ssd(z, xbc, dtp, dtb, alog, dexp, nw, e3, batch):
    t = z.shape[0]
    nc = t // batch // CHUNK
    row = lambda w: pl.BlockSpec((CHUNK, w), lambda b, c: (b * nc + c, 0))
    return pl.pallas_call(
        _ssd_kernel,
        grid=(batch, nc),
        in_specs=[row(SSD_D_INNER), row(SSD_CONV_DIM), row(DT_PAD),
                  _resident((1, DT_PAD)), _resident((1, DT_PAD)), _resident((1, SSD_D_INNER)),
                  _resident((1, SSD_D_INNER)), _resident((LANES, SSD_D_INNER))],
        out_specs=row(SSD_D_INNER),
        out_shape=jax.ShapeDtypeStruct((t, SSD_D_INNER), BF16),
        scratch_shapes=[pltpu.VMEM((SSD_N_GROUPS, SSD_D_STATE, SSD_GROUP_WIDTH), F32),
                        pltpu.VMEM((CHUNK, SSD_D_INNER), F32)],
        compiler_params=pltpu.CompilerParams(dimension_semantics=("arbitrary", "arbitrary"),
                                             vmem_limit_bytes=VMEM_LIMIT),
        name="ssd",
    )(z, xbc, dtp, dtb, alog, dexp, nw, e3)


ROPE_HALF = ATTN_HEAD_DIM // 2


def _attn_kernel(sink_ref, qkvt_ref, pos_ref, invf_ref, aot_ref, qt, kk, vvt):
    L = CHUNK
    blk = pl.program_id(1)

    @pl.when(blk == 0)
    def _():
        kk[:, L:2 * L, :] = jnp.zeros((ATTN_N_KV_HEADS, L, ATTN_HEAD_DIM), BF16)
        vvt[:, L:2 * L] = jnp.zeros((KV_WIDTH, L), BF16)

    @pl.when(blk > 0)
    def _():
        kk[:, L:2 * L, :] = kk[:, 0:L, :]
        vvt[:, L:2 * L] = vvt[:, 0:L]

    ang = invf_ref[...] * pos_ref[...].astype(F32)
    cosv = jnp.cos(ang)
    sinv = jnp.sin(ang)

    def rope(r0):
        t1 = qkvt_ref[r0:r0 + ROPE_HALF, :].astype(F32)
        t2 = qkvt_ref[r0 + ROPE_HALF:r0 + ATTN_HEAD_DIM, :].astype(F32)
        return jnp.concatenate([t1 * cosv - t2 * sinv, t2 * cosv + t1 * sinv], axis=0)

    scale = ATTN_HEAD_DIM ** -0.5
    for h in range(ATTN_N_HEADS):
        r0 = h * ATTN_HEAD_DIM
        qt[r0:r0 + ATTN_HEAD_DIM, :] = (rope(r0) * scale).astype(BF16)
    for g in range(ATTN_N_KV_HEADS):
        kk[g, 0:L, :] = rope(ATTN_WIDTH + g * ATTN_HEAD_DIM).T.astype(BF16)
    vvt[:, 0:L] = qkvt_ref[ATTN_WIDTH + KV_WIDTH:QKV_WIDTH, :]

    cols = ATTN_REP * L
    key_i = lax.broadcasted_iota(jnp.int32, (L, cols), 0)
    qry_i = lax.broadcasted_iota(jnp.int32, (L, cols), 1) % L
    in_cur = key_i <= qry_i
    old_bias = jnp.where(blk > 0, 0.0, -jnp.inf)
    for g in range(ATTN_N_KV_HEADS):
        heads = range(g * ATTN_REP, (g + 1) * ATTN_REP)
        qcat = jnp.concatenate([qt[h * ATTN_HEAD_DIM:(h + 1) * ATTN_HEAD_DIM, :] for h in heads], axis=1)
        s2 = jnp.dot(kk[g], qcat, preferred_element_type=F32)
        s = jnp.where(in_cur, s2[0:L], s2[L:2 * L] + old_bias)
        sink = jnp.concatenate([jnp.full((1, L), sink_ref[h], F32) for h in heads], axis=1)
        m = jnp.maximum(jnp.max(s, axis=0, keepdims=True), sink)
        p = jnp.exp(s - m)
        denom = jnp.sum(p, axis=0, keepdims=True) + jnp.exp(sink - m)
        pt = jnp.concatenate([jnp.where(in_cur, p, 0.0), jnp.where(in_cur, 0.0, p)], axis=0).astype(BF16)
        vt = vvt[g * ATTN_HEAD_DIM:(g + 1) * ATTN_HEAD_DIM, :]
        o = jnp.dot(vt, pt, preferred_element_type=F32) * (1.0 / denom)
        for r, h in enumerate(heads):
            aot_ref[h * ATTN_HEAD_DIM:(h + 1) * ATTN_HEAD_DIM, :] = o[:, r * L:(r + 1) * L].astype(BF16)


def _attn(sinks, qkvt, pos_row, invf, batch):
    t = qkvt.shape[1]
    nb = t // batch // CHUNK
    col = lambda h: pl.BlockSpec((h, CHUNK), lambda b, c: (0, b * nb + c))
    return pl.pallas_call(
        _attn_kernel,
        grid=(batch, nb),
        in_specs=[pl.BlockSpec(memory_space=pltpu.SMEM), col(QKV_WIDTH), col(1), _resident((ROPE_HALF, CHUNK))],
        out_specs=col(ATTN_WIDTH),
        out_shape=jax.ShapeDtypeStruct((ATTN_WIDTH, t), BF16),
        scratch_shapes=[pltpu.VMEM((ATTN_WIDTH, CHUNK), BF16),
                        pltpu.VMEM((ATTN_N_KV_HEADS, 2 * CHUNK, ATTN_HEAD_DIM), BF16),
                        pltpu.VMEM((KV_WIDTH, 2 * CHUNK), BF16)],
        compiler_params=pltpu.CompilerParams(dimension_semantics=("arbitrary", "arbitrary"),
                                             vmem_limit_bytes=VMEM_LIMIT),
        name="attn",
    )(sinks, qkvt, pos_row, invf)


def _mix_out_kernel(yn_ref, aot_ref, gates_ref, x_ref, wso_ref, wao_ref, wmix_ref, nw_ref, o_ref):
    ys = jnp.dot(yn_ref[...], wso_ref[...], preferred_element_type=F32)
    ya = lax.dot_general(aot_ref[...], wao_ref[...], _TN, preferred_element_type=F32)
    gs = _sigmoid(gates_ref[:, 0:D_MODEL].astype(F32))
    ga = _sigmoid(gates_ref[:, D_MODEL:GATES_WIDTH].astype(F32))
    merged = (gs * ys + ga * ya).astype(BF16)
    mo = jnp.dot(merged, wmix_ref[...], preferred_element_type=F32)
    ms = jnp.mean(mo * mo, axis=-1, keepdims=True)
    o_ref[...] = x_ref[...] + mo * lax.rsqrt(ms + NORM_EPS) * nw_ref[...]


def _mix_out(yn, ao, gates, x2, wso, wao, wmix, nw, tm):
    t = x2.shape[0]
    row = lambda w: pl.BlockSpec((tm, w), lambda i: (i, 0))
    return pl.pallas_call(
        _mix_out_kernel,
        grid=(t // tm,),
        in_specs=[row(SSD_D_INNER), pl.BlockSpec((ATTN_WIDTH, tm), lambda i: (0, i)), row(GATES_WIDTH), row(D_MODEL),
                  _resident((SSD_D_INNER, D_MODEL)), _resident((ATTN_WIDTH, D_MODEL)),
                  _resident((D_MODEL, D_MODEL)), _resident((1, D_MODEL))],
        out_specs=row(D_MODEL),
        out_shape=jax.ShapeDtypeStruct((t, D_MODEL), F32),
        compiler_params=pltpu.CompilerParams(dimension_semantics=("arbitrary",), vmem_limit_bytes=VMEM_LIMIT),
        name="mix_out",
    )(yn, ao, gates, x2, wso, wao, wmix, nw)


FFN_NCHUNK = 256
FFN_SUBTILE = 256
_GELU_A = float(-2.0 * np.sqrt(2.0 / np.pi) * np.log2(np.e))
_GELU_B = float(_GELU_A * 0.044715)


def _gelu_tanh(v):
    return v / (1.0 + jnp.exp2(v * (_GELU_A + _GELU_B * (v * v))))


def _ffn_kernel(x_ref, npre_ref, wup_ref, cw_ref, cb_ref, wdn_ref, npost_ref, o_ref, carry, act,
                *, tiles_per_seq):
    tm = x_ref.shape[0]
    ts = min(FFN_SUBTILE, tm)

    @pl.when(pl.program_id(0) % tiles_per_seq == 0)
    def _():
        carry[...] = jnp.zeros(carry.shape, F32)

    def up(h, c0):
        return (jnp.dot(h, wup_ref[:, c0:c0 + FFN_NCHUNK], preferred_element_type=F32),
                jnp.dot(h, wup_ref[:, FFN_D_FF + c0:FFN_D_FF + c0 + FFN_NCHUNK], preferred_element_type=F32))

    def conv_chunk(r, c0):
        cs = slice(c0, c0 + FFN_NCHUNK)
        out = _causal_conv_rows(r, carry[:, cs], cw_ref[:, cs], cb_ref[:, cs])
        carry[:, cs] = r[ts - CONV_HALO:ts, :]
        return out

    chunks = list(range(0, FFN_D_FF, FFN_NCHUNK))
    for r0 in range(0, tm, ts):
        x = x_ref[r0:r0 + ts, :]
        ms = jnp.mean(x * x, axis=-1, keepdims=True)
        h = (x * lax.rsqrt(ms + NORM_EPS) * npre_ref[...]).astype(BF16)
        nxt = up(h, chunks[0])
        for i, c0 in enumerate(chunks):
            cur = nxt
            if i + 1 < len(chunks):
                nxt = up(h, chunks[i + 1])
            gate = conv_chunk(cur[0], c0)
            val = conv_chunk(cur[1], FFN_D_FF + c0)
            act[r0:r0 + ts, c0:c0 + FFN_NCHUNK] = (_gelu_tanh(gate) * val).astype(BF16)
        ff = jnp.dot(act[r0:r0 + ts, :], wdn_ref[...], preferred_element_type=F32)
        ms2 = jnp.mean(ff * ff, axis=-1, keepdims=True)
        o_ref[r0:r0 + ts, :] = x + ff * lax.rsqrt(ms2 + NORM_EPS) * npost_ref[...]


def _ffn(x1, npre, wup, cw, cb, wdn, npost, tm, batch):
    t = x1.shape[0]
    row = lambda w: pl.BlockSpec((tm, w), lambda i: (i, 0))
    return pl.pallas_call(
        functools.partial(_ffn_kernel, tiles_per_seq=t // batch // tm),
        grid=(t // tm,),
        in_specs=[row(D_MODEL), _resident((1, D_MODEL)), _resident((D_MODEL, 2 * FFN_D_FF)),
                  _resident((FFN_CONV_WIDTH, 2 * FFN_D_FF)), _resident((1, 2 * FFN_D_FF)),
                  _resident((FFN_D_FF, D_MODEL)), _resident((1, D_MODEL))],
        out_specs=row(D_MODEL),
        out_shape=jax.ShapeDtypeStruct((t, D_MODEL), F32),
        scratch_shapes=[pltpu.VMEM((CONV_HALO, 2 * FFN_D_FF), F32),
                        pltpu.VMEM((tm, FFN_D_FF), BF16)],
        compiler_params=pltpu.CompilerParams(dimension_semantics=("arbitrary",), vmem_limit_bytes=VMEM_LIMIT),
        name="ffn",
    )(x1, npre, wup, cw, cb, wdn, npost)


def _expansion_matrix():
    e = np.zeros((LANES, SSD_D_INNER), np.float32)
    ch = np.arange(SSD_D_INNER)
    for part in range(3):
        e[part * SSD_N_HEADS + ch // SSD_HEAD_DIM, ch] = 1.0
    return jnp.asarray(e, dtype=BF16)


def _rope_inv_freq():
    inv = ROPE_THETA ** (-jnp.arange(ROPE_HALF, dtype=F32) * 2.0 / ATTN_HEAD_DIM)
    return jnp.broadcast_to(inv[:, None], (ROPE_HALF, CHUNK))


def _layer(x2, pos_row, batch, norm_mix_pre_w, w_in, ssd_conv_w, ssd_conv_b, ssd_dt_bias, ssd_a_log, ssd_d,
           ssd_norm_w, ssd_w_out, attn_sinks, attn_w_out, w_mix_out, norm_mix_post_w, norm_ffn_pre_w,
           ffn_w_up, ffn_conv_w, ffn_conv_b, ffn_w_down, norm_ffn_post_w, tm):
    o = np.cumsum((0, SSD_D_INNER, SSD_CONV_DIM, SSD_N_HEADS, ATTN_WIDTH, KV_WIDTH, KV_WIDTH, D_MODEL, D_MODEL))
    w_all = jnp.concatenate([w_in[:, o[0]:o[2]], w_in[:, o[6]:o[8]], w_in[:, o[2]:o[3]],
                             jnp.zeros((D_MODEL, DT_PAD - SSD_N_HEADS), w_in.dtype)], axis=1).astype(BF16)
    w_t = w_in[:, o[3]:o[6]].T.astype(BF16)
    row = lambda v: v.reshape(1, -1).astype(F32)
    pad_heads = lambda v: jnp.pad(v.astype(F32), (0, DT_PAD - SSD_N_HEADS)).reshape(1, DT_PAD)

    z, xbc, gates, dtp, qkvt = _in_proj(x2, row(norm_mix_pre_w), w_all, w_t, ssd_conv_w.astype(F32),
                                        row(ssd_conv_b), tm, batch)
    yn = _ssd(z, xbc, dtp, pad_heads(ssd_dt_bias), pad_heads(ssd_a_log),
              row(jnp.repeat(ssd_d, SSD_HEAD_DIM)), row(ssd_norm_w), _expansion_matrix(), batch)
    aot = _attn(attn_sinks.astype(F32), qkvt, pos_row, _rope_inv_freq(), batch)
    x1 = _mix_out(yn, aot, gates, x2, ssd_w_out.astype(BF16), attn_w_out.astype(BF16), w_mix_out.astype(BF16),
                  row(norm_mix_post_w), tm)
    return _ffn(x1, row(norm_ffn_pre_w), ffn_w_up.astype(BF16), ffn_conv_w.astype(F32), row(ffn_conv_b),
                ffn_w_down.astype(BF16), row(norm_ffn_post_w), tm, batch)


def kernel(x, positions, norm_mix_pre_w, w_in, ssd_conv_w, ssd_conv_b, ssd_dt_bias, ssd_a_log, ssd_d, ssd_norm_w,
           ssd_w_out, attn_sinks, attn_w_out, w_mix_out, norm_mix_post_w, norm_ffn_pre_w, ffn_w_up, ffn_conv_w,
           ffn_conv_b, ffn_w_down, norm_ffn_post_w):
    batch, seq, d = x.shape
    assert d == D_MODEL and seq % CHUNK == 0
    tm = 512 if seq % 512 == 0 else CHUNK
    x2 = x.reshape(batch * seq, d)
    pos_row = positions.reshape(1, batch * seq)
    for i in range(w_in.shape[0]):
        x2 = _layer(x2, pos_row, batch, norm_mix_pre_w[i], w_in[i], ssd_conv_w[i], ssd_conv_b[i], ssd_dt_bias[i],
                    ssd_a_log[i], ssd_d[i], ssd_norm_w[i], ssd_w_out[i], attn_sinks[i], attn_w_out[i],
                    w_mix_out[i], norm_mix_post_w[i], norm_ffn_pre_w[i], ffn_w_up[i], ffn_conv_w[i],
                    ffn_conv_b[i], ffn_w_down[i], norm_ffn_post_w[i], tm)
    return x2.reshape(batch, seq, d)
```

```python
import functools

import numpy as np
import jax
import jax.numpy as jnp
from jax import lax
from jax.experimental import pallas as pl
from jax.experimental.pallas import tpu as pltpu

F32 = jnp.float32
BF16 = jnp.bfloat16

D_MODEL = 1024
SSD_D_INNER = 2048
SSD_HEAD_DIM = 64
SSD_N_HEADS = 32
SSD_N_GROUPS = 4
SSD_HEADS_PER_GROUP = 8
SSD_D_STATE = 128
SSD_CONV_WIDTH = 4
SSD_CONV_DIM = 3072
SSD_GROUP_WIDTH = SSD_D_INNER // SSD_N_GROUPS
CHUNK = 128

ATTN_HEAD_DIM = 64
ATTN_N_HEADS = 16
ATTN_N_KV_HEADS = 4
ATTN_REP = 4
ATTN_WIDTH = 1024
KV_WIDTH = 256
QKV_WIDTH = ATTN_WIDTH + 2 * KV_WIDTH
ROPE_THETA = 10000.0

FFN_D_FF = 2816
FFN_CONV_WIDTH = 3
NORM_EPS = 1e-6

LANES = 128
SUBLANES = 8
DT_PAD = LANES
GATES_WIDTH = 2 * D_MODEL

VMEM_LIMIT = 56 * 1024 * 1024


def _resident(shape):
    nd = len(shape)
    return pl.BlockSpec(shape, lambda *_: (0,) * nd, pipeline_mode=pl.Buffered(1))


def _sigmoid(v):
    return 1.0 / (1.0 + jnp.exp(-v))


def _split3(v):
    hi = v.astype(BF16).astype(F32)
    r = v - hi
    mid = r.astype(BF16).astype(F32)
    lo = (r - mid).astype(BF16).astype(F32)
    return hi, mid, lo


IN_PROJ_NCHUNK = 512
IN_PROJ_CONV_CHUNK = 256


IN_PROJ_TCHUNK = 256
_NT = (((1,), (1,)), ((), ()))
_TN = (((0,), (0,)), ((), ()))


CONV_HALO = SUBLANES
_LOG2E = float(np.log2(np.e))


def _silu(v):
    return v / (1.0 + jnp.exp2(v * -_LOG2E))


def _causal_conv_rows(r, halo, w, b):
    taps, (rows, width) = w.shape[0], r.shape
    nblk = rows // SUBLANES
    full = jnp.concatenate([halo, r], axis=0).reshape(nblk + 1, SUBLANES, width)
    sub = lax.broadcasted_iota(jnp.int32, (nblk, SUBLANES, width), 1)
    out = b + w[taps - 1:taps, :] * r
    for s in range(1, taps):
        merged = jnp.where(sub >= SUBLANES - s, full[0:nblk], full[1:nblk + 1])
        shifted = pltpu.roll(merged, s, 1).reshape(rows, width)
        out = out + w[taps - 1 - s:taps - s, :] * shifted
    return out


def _in_proj_kernel(x_ref, nw_ref, wzx_ref, wg_ref, wdt_ref, wqkv_ref, cw_ref, cb_ref,
                    z_ref, xbc_ref, gates_ref, dt_ref, qkvt_ref, carry, wt, *, tiles_per_seq):
    tm = x_ref.shape[0]

    @pl.when(pl.program_id(0) == 0)
    def _():
        for c in range(0, QKV_WIDTH, IN_PROJ_TCHUNK):
            wt[c:c + IN_PROJ_TCHUNK, :] = wqkv_ref[:, c:c + IN_PROJ_TCHUNK].astype(F32).T.astype(BF16)

    @pl.when(pl.program_id(0) % tiles_per_seq == 0)
    def _():
        carry[...] = jnp.zeros(carry.shape, F32)

    x = x_ref[...]
    ms = jnp.mean(x * x, axis=-1, keepdims=True)
    u = (x * lax.rsqrt(ms + NORM_EPS) * nw_ref[...]).astype(BF16)
    def plain(out_ref, w_ref, c, step):
        r = jnp.dot(u, w_ref[:, c:c + step], preferred_element_type=F32)
        out_ref[:, c:c + step] = r.astype(out_ref.dtype)

    def transposed(r0):
        r = lax.dot_general(wt[r0:r0 + IN_PROJ_TCHUNK, :], u, _NT, preferred_element_type=F32)
        qkvt_ref[r0:r0 + IN_PROJ_TCHUNK, :] = r.astype(BF16)

    fillers = [functools.partial(plain, out_ref, w_ref, c, min(IN_PROJ_NCHUNK, width))
               for out_ref, w_ref, width in ((z_ref, wzx_ref, SSD_D_INNER), (gates_ref, wg_ref, GATES_WIDTH),
                                             (dt_ref, wdt_ref, DT_PAD))
               for c in range(0, width, min(IN_PROJ_NCHUNK, width))]
    fillers += [functools.partial(transposed, r0) for r0 in range(0, QKV_WIDTH, IN_PROJ_TCHUNK)]
    n_conv = SSD_CONV_DIM // IN_PROJ_CONV_CHUNK
    per_conv = -(-len(fillers) // n_conv)

    for i in range(n_conv):
        c = i * IN_PROJ_CONV_CHUNK
        cs = slice(c, c + IN_PROJ_CONV_CHUNK)
        r = jnp.dot(u, wzx_ref[:, SSD_D_INNER + c:SSD_D_INNER + c + IN_PROJ_CONV_CHUNK],
                    preferred_element_type=F32)
        for f in fillers[i * per_conv:(i + 1) * per_conv]:
            f()
        acc = _causal_conv_rows(r, carry[:, cs], cw_ref[:, cs], cb_ref[:, cs])
        carry[:, cs] = r[tm - CONV_HALO:tm, :]
        xbc_ref[:, cs] = _silu(acc).astype(BF16)


def _in_proj(x2, nw, w_zx, w_g, w_dt, w_qkv, cw, cb, tm, batch):
    t = x2.shape[0]
    row = lambda w: pl.BlockSpec((tm, w), lambda i: (i, 0))
    return pl.pallas_call(
        functools.partial(_in_proj_kernel, tiles_per_seq=t // batch // tm),
        grid=(t // tm,),
        in_specs=[row(D_MODEL), _resident((1, D_MODEL)), _resident((D_MODEL, SSD_D_INNER + SSD_CONV_DIM)),
                  _resident((D_MODEL, GATES_WIDTH)), _resident((D_MODEL, DT_PAD)), _resident((D_MODEL, QKV_WIDTH)),
                  _resident((SSD_CONV_WIDTH, SSD_CONV_DIM)), _resident((1, SSD_CONV_DIM))],
        out_specs=[row(SSD_D_INNER), row(SSD_CONV_DIM), row(GATES_WIDTH), row(DT_PAD),
                   pl.BlockSpec((QKV_WIDTH, tm), lambda i: (0, i))],
        out_shape=[jax.ShapeDtypeStruct((t, SSD_D_INNER), BF16), jax.ShapeDtypeStruct((t, SSD_CONV_DIM), BF16),
                   jax.ShapeDtypeStruct((t, GATES_WIDTH), BF16), jax.ShapeDtypeStruct((t, DT_PAD), F32),
                   jax.ShapeDtypeStruct((QKV_WIDTH, t), BF16)],
        scratch_shapes=[pltpu.VMEM((CONV_HALO, SSD_CONV_DIM), F32), pltpu.VMEM((QKV_WIDTH, D_MODEL), BF16)],
        compiler_params=pltpu.CompilerParams(dimension_semantics=("arbitrary",), vmem_limit_bytes=VMEM_LIMIT),
        name="in_proj",
    )(x2, nw, w_zx, w_g, w_dt, w_qkv, cw, cb)


_COL_B = SSD_D_INNER
_COL_C = SSD_D_INNER + SSD_N_GROUPS * SSD_D_STATE


def _ssd_kernel(z_ref, xbc_ref, dt_ref, dtb_ref, alog_ref, dexp_ref, nw_ref, e3_ref, yn_ref, state, yacc):
    L = CHUNK

    @pl.when(pl.program_id(1) == 0)
    def _():
        state[...] = jnp.zeros(state.shape, F32)

    lane = lax.broadcasted_iota(jnp.int32, (L, LANES), 1)
    head_lane = lane < SSD_N_HEADS
    dtr = dt_ref[...] + dtb_ref[...]
    dt = jnp.maximum(dtr, 0.0) + jnp.log(1.0 + jnp.exp(-jnp.abs(dtr)))
    dt = jnp.where(head_lane, dt, 0.0)
    adt = dt * (-jnp.exp(alog_ref[...]))
    row_i = lax.broadcasted_iota(jnp.int32, (L, L), 0)
    col_i = lax.broadcasted_iota(jnp.int32, (L, L), 1)
    causal = col_i <= row_i
    tril = jnp.where(causal, 1.0, 0.0).astype(BF16)
    acs = sum(jnp.dot(tril, p.astype(BF16), preferred_element_type=F32) for p in _split3(adt)) * _LOG2E
    acs_t = acs.T
    dt_t = dt.T

    def pack3(v):
        hi, mid, lo = _split3(v)
        return jnp.where(head_lane, hi, jnp.where(lane < 2 * SSD_N_HEADS, pltpu.roll(mid, SSD_N_HEADS, 1),
                                                  pltpu.roll(lo, 2 * SSD_N_HEADS, 1))).astype(BF16)

    e_h = jnp.exp2(acs)
    w_h = dt * jnp.exp2(acs[L - 1:L, :] - acs)
    exp_in = jnp.concatenate([pack3(e_h), pack3(w_h)], axis=0)
    expd = jnp.dot(exp_in, e3_ref[...], preferred_element_type=F32)
    e_x = expd[0:L, :]
    w_x = expd[L:2 * L, :]
    cd_x = e_x[L - 1:L, :]

    for g in range(SSD_N_GROUPS):
        gs = slice(g * SSD_GROUP_WIDTH, (g + 1) * SSD_GROUP_WIDTH)
        xb = xbc_ref[:, gs]
        xg = xb.astype(F32)
        bm = xbc_ref[:, _COL_B + g * SSD_D_STATE:_COL_B + (g + 1) * SSD_D_STATE]
        cm = xbc_ref[:, _COL_C + g * SSD_D_STATE:_COL_C + (g + 1) * SSD_D_STATE]
        st = state[g]
        yoff = jnp.dot(cm, st.astype(BF16), preferred_element_type=F32) * e_x[:, gs]
        xd = (xg * w_x[:, gs]).astype(BF16)
        upd = lax.dot_general(bm, xd, _TN, preferred_element_type=F32)
        state[g] = st * cd_x[:, gs] + upd
        yacc[:, gs] = yoff + dexp_ref[:, gs] * xg
        cbm = lax.dot_general(cm, bm, _NT, preferred_element_type=F32)
        for jp in range(0, SSD_HEADS_PER_GROUP, 2):
            pair = []
            for j in (jp, jp + 1):
                h = g * SSD_HEADS_PER_GROUP + j
                seg = acs[:, h:h + 1] - acs_t[h:h + 1, :]
                m = jnp.where(causal, jnp.exp2(seg), 0.0) * cbm * dt_t[h:h + 1, :]
                pair.append(jnp.dot(m.astype(BF16), xb[:, j * SSD_HEAD_DIM:(j + 1) * SSD_HEAD_DIM],
                                    preferred_element_type=F32))
            c0 = g * SSD_GROUP_WIDTH + jp * SSD_HEAD_DIM
            yacc[:, c0:c0 + LANES] += jnp.concatenate(pair, axis=1)

    for g in range(SSD_N_GROUPS):
        gs = slice(g * SSD_GROUP_WIDTH, (g + 1) * SSD_GROUP_WIDTH)
        gv = yacc[:, gs] * _silu(z_ref[:, gs].astype(F32))
        ms = jnp.mean(gv * gv, axis=-1, keepdims=True)
        yn_ref[:, gs] = (gv * lax.rsqrt(ms + NORM_EPS) * nw_ref[:, gs]).astype(BF16)


def _ssd(z, xbc, dtp, dtb, alog, dexp, nw, e3, batch):
    t = z.shape[0]
    nc = t // batch // CHUNK
    row = lambda w: pl.BlockSpec((CHUNK, w), lambda b, c: (b * nc + c, 0))
    return pl.pallas_call(
        _ssd_kernel,
        grid=(batch, nc),
        in_specs=[row(SSD_D_INNER), row(SSD_CONV_DIM), row(DT_PAD),
                  _resident((1, DT_PAD)), _resident((1, DT_PAD)), _resident((1, SSD_D_INNER)),
                  _resident((1, SSD_D_INNER)), _resident((LANES, SSD_D_INNER))],
        out_specs=row(SSD_D_INNER),
        out_shape=jax.ShapeDtypeStruct((t, SSD_D_INNER), BF16),
        scratch_shapes=[pltpu.VMEM((SSD_N_GROUPS, SSD_D_STATE, SSD_GROUP_WIDTH), F32),
                        pltpu.VMEM((CHUNK, SSD_D_INNER), F32)],
        compiler_params=pltpu.CompilerParams(dimension_semantics=("arbitrary", "arbitrary"),
                                             vmem_limit_bytes=VMEM_LIMIT),
        name="ssd",
    )(z, xbc, dtp, dtb, alog, dexp, nw, e3)


ROPE_HALF = ATTN_HEAD_DIM // 2


def _attn_kernel(sink_ref, qkvt_ref, pos_ref, invf_ref, aot_ref, qt, kk, vvt):
    L = CHUNK
    blk = pl.program_id(1)

    @pl.when(blk == 0)
    def _():
        kk[:, L:2 * L, :] = jnp.zeros((ATTN_N_KV_HEADS, L, ATTN_HEAD_DIM), BF16)
        vvt[:, L:2 * L] = jnp.zeros((KV_WIDTH, L), BF16)

    @pl.when(blk > 0)
    def _():
        kk[:, L:2 * L, :] = kk[:, 0:L, :]
        vvt[:, L:2 * L] = vvt[:, 0:L]

    ang = invf_ref[...] * pos_ref[...].astype(F32)
    cosv = jnp.cos(ang)
    sinv = jnp.sin(ang)

    def rope(r0):
        t1 = qkvt_ref[r0:r0 + ROPE_HALF, :].astype(F32)
        t2 = qkvt_ref[r0 + ROPE_HALF:r0 + ATTN_HEAD_DIM, :].astype(F32)
        return jnp.concatenate([t1 * cosv - t2 * sinv, t2 * cosv + t1 * sinv], axis=0)

    scale = ATTN_HEAD_DIM ** -0.5 * _LOG2E
    for h in range(ATTN_N_HEADS):
        r0 = h * ATTN_HEAD_DIM
        qt[r0:r0 + ATTN_HEAD_DIM, :] = (rope(r0) * scale).astype(BF16)
    for g in range(ATTN_N_KV_HEADS):
        kk[g, 0:L, :] = rope(ATTN_WIDTH + g * ATTN_HEAD_DIM).T.astype(BF16)
    vvt[:, 0:L] = qkvt_ref[ATTN_WIDTH + KV_WIDTH:QKV_WIDTH, :]

    cols = ATTN_REP * L
    key_i = lax.broadcasted_iota(jnp.int32, (L, cols), 0)
    qry_i = lax.broadcasted_iota(jnp.int32, (L, cols), 1) % L
    in_cur = key_i <= qry_i
    old_bias = jnp.where(blk > 0, 0.0, -jnp.inf)
    groups = range(ATTN_N_KV_HEADS)
    heads = [range(g * ATTN_REP, (g + 1) * ATTN_REP) for g in groups]
    s2 = [jnp.dot(kk[g], jnp.concatenate([qt[h * ATTN_HEAD_DIM:(h + 1) * ATTN_HEAD_DIM, :] for h in heads[g]],
                                         axis=1), preferred_element_type=F32) for g in groups]
    s = [jnp.where(in_cur, s2[g][0:L], s2[g][L:2 * L] + old_bias) for g in groups]
    sink = [jnp.concatenate([jnp.full((1, L), sink_ref[h] * _LOG2E, F32) for h in heads[g]], axis=1) for g in groups]
    m = [jnp.maximum(jnp.max(s[g], axis=0, keepdims=True), sink[g]) for g in groups]
    p = [jnp.exp2(s[g] - m[g]) for g in groups]
    denom = [jnp.sum(p[g], axis=0, keepdims=True) + jnp.exp2(sink[g] - m[g]) for g in groups]
    pt = [jnp.concatenate([jnp.where(in_cur, p[g], 0.0), jnp.where(in_cur, 0.0, p[g])], axis=0).astype(BF16)
          for g in groups]
    o = [jnp.dot(vvt[g * ATTN_HEAD_DIM:(g + 1) * ATTN_HEAD_DIM, :], pt[g], preferred_element_type=F32)
         * (1.0 / denom[g]) for g in groups]
    for g in groups:
        for r, h in enumerate(heads[g]):
            aot_ref[h * ATTN_HEAD_DIM:(h + 1) * ATTN_HEAD_DIM, :] = o[g][:, r * L:(r + 1) * L].astype(BF16)


def _attn(sinks, qkvt, pos_row, invf, batch):
    t = qkvt.shape[1]
    nb = t // batch // CHUNK
    col = lambda h: pl.BlockSpec((h, CHUNK), lambda b, c: (0, b * nb + c))
    return pl.pallas_call(
        _attn_kernel,
        grid=(batch, nb),
        in_specs=[pl.BlockSpec(memory_space=pltpu.SMEM), col(QKV_WIDTH), col(1), _resident((ROPE_HALF, CHUNK))],
        out_specs=col(ATTN_WIDTH),
        out_shape=jax.ShapeDtypeStruct((ATTN_WIDTH, t), BF16),
        scratch_shapes=[pltpu.VMEM((ATTN_WIDTH, CHUNK), BF16),
                        pltpu.VMEM((ATTN_N_KV_HEADS, 2 * CHUNK, ATTN_HEAD_DIM), BF16),
                        pltpu.VMEM((KV_WIDTH, 2 * CHUNK), BF16)],
        compiler_params=pltpu.CompilerParams(dimension_semantics=("arbitrary", "arbitrary"),
                                             vmem_limit_bytes=VMEM_LIMIT),
        name="attn",
    )(sinks, qkvt, pos_row, invf)


def _mix_out_kernel(yn_ref, aot_ref, gates_ref, x_ref, wso_ref, wao_ref, wmix_ref, nw_ref, o_ref):
    ys = jnp.dot(yn_ref[...], wso_ref[...], preferred_element_type=F32)
    ya = lax.dot_general(aot_ref[...], wao_ref[...], _TN, preferred_element_type=F32)
    gs = _sigmoid(gates_ref[:, 0:D_MODEL].astype(F32))
    ga = _sigmoid(gates_ref[:, D_MODEL:GATES_WIDTH].astype(F32))
    merged = (gs * ys + ga * ya).astype(BF16)
    mo = jnp.dot(merged, wmix_ref[...], preferred_element_type=F32)
    ms = jnp.mean(mo * mo, axis=-1, keepdims=True)
    o_ref[...] = x_ref[...] + mo * lax.rsqrt(ms + NORM_EPS) * nw_ref[...]


def _mix_out(yn, ao, gates, x2, wso, wao, wmix, nw, tm):
    t = x2.shape[0]
    row = lambda w: pl.BlockSpec((tm, w), lambda i: (i, 0))
    return pl.pallas_call(
        _mix_out_kernel,
        grid=(t // tm,),
        in_specs=[row(SSD_D_INNER), pl.BlockSpec((ATTN_WIDTH, tm), lambda i: (0, i)), row(GATES_WIDTH), row(D_MODEL),
                  _resident((SSD_D_INNER, D_MODEL)), _resident((ATTN_WIDTH, D_MODEL)),
                  _resident((D_MODEL, D_MODEL)), _resident((1, D_MODEL))],
        out_specs=row(D_MODEL),
        out_shape=jax.ShapeDtypeStruct((t, D_MODEL), F32),
        compiler_params=pltpu.CompilerParams(dimension_semantics=("arbitrary",), vmem_limit_bytes=VMEM_LIMIT),
        name="mix_out",
    )(yn, ao, gates, x2, wso, wao, wmix, nw)


FFN_NCHUNK = 256
FFN_SUBTILE = 256
_GELU_A = float(-2.0 * np.sqrt(2.0 / np.pi) * np.log2(np.e))
_GELU_B = float(_GELU_A * 0.044715)


def _gelu_tanh(v):
    return v / (1.0 + jnp.exp2(v * (_GELU_A + _GELU_B * (v * v))))


def _ffn_kernel(x_ref, npre_ref, wup_ref, cw_ref, cb_ref, wdn_ref, npost_ref, o_ref, carry, act,
                *, tiles_per_seq):
    tm = x_ref.shape[0]
    ts = min(FFN_SUBTILE, tm)

    @pl.when(pl.program_id(0) % tiles_per_seq == 0)
    def _():
        carry[...] = jnp.zeros(carry.shape, F32)

    def up(h, c0):
        return (jnp.dot(h, wup_ref[:, c0:c0 + FFN_NCHUNK], preferred_element_type=F32),
                jnp.dot(h, wup_ref[:, FFN_D_FF + c0:FFN_D_FF + c0 + FFN_NCHUNK], preferred_element_type=F32))

    def conv_chunk(r, c0):
        cs = slice(c0, c0 + FFN_NCHUNK)
        out = _causal_conv_rows(r, carry[:, cs], cw_ref[:, cs], cb_ref[:, cs])
        carry[:, cs] = r[ts - CONV_HALO:ts, :]
        return out

    chunks = list(range(0, FFN_D_FF, FFN_NCHUNK))
    for r0 in range(0, tm, ts):
        x = x_ref[r0:r0 + ts, :]
        ms = jnp.mean(x * x, axis=-1, keepdims=True)
        h = (x * lax.rsqrt(ms + NORM_EPS) * npre_ref[...]).astype(BF16)
        nxt = up(h, chunks[0])
        for i, c0 in enumerate(chunks):
            cur = nxt
            if i + 1 < len(chunks):
                nxt = up(h, chunks[i + 1])
            gate = conv_chunk(cur[0], c0).astype(BF16)
            val = conv_chunk(cur[1], FFN_D_FF + c0).astype(BF16)
            act[r0:r0 + ts, c0:c0 + FFN_NCHUNK] = _gelu_tanh(gate) * val
        ff = jnp.dot(act[r0:r0 + ts, :], wdn_ref[...], preferred_element_type=F32)
        ms2 = jnp.mean(ff * ff, axis=-1, keepdims=True)
        o_ref[r0:r0 + ts, :] = x + ff * lax.rsqrt(ms2 + NORM_EPS) * npost_ref[...]


def _ffn(x1, npre, wup, cw, cb, wdn, npost, tm, batch):
    t = x1.shape[0]
    row = lambda w: pl.BlockSpec((tm, w), lambda i: (i, 0))
    return pl.pallas_call(
        functools.partial(_ffn_kernel, tiles_per_seq=t // batch // tm),
        grid=(t // tm,),
        in_specs=[row(D_MODEL), _resident((1, D_MODEL)), _resident((D_MODEL, 2 * FFN_D_FF)),
                  _resident((FFN_CONV_WIDTH, 2 * FFN_D_FF)), _resident((1, 2 * FFN_D_FF)),
                  _resident((FFN_D_FF, D_MODEL)), _resident((1, D_MODEL))],
        out_specs=row(D_MODEL),
        out_shape=jax.ShapeDtypeStruct((t, D_MODEL), F32),
        scratch_shapes=[pltpu.VMEM((CONV_HALO, 2 * FFN_D_FF), F32),
                        pltpu.VMEM((tm, FFN_D_FF), BF16)],
        compiler_params=pltpu.CompilerParams(dimension_semantics=("arbitrary",), vmem_limit_bytes=VMEM_LIMIT),
        name="ffn",
    )(x1, npre, wup, cw, cb, wdn, npost)


def _expansion_matrix():
    e = np.zeros((LANES, SSD_D_INNER), np.float32)
    ch = np.arange(SSD_D_INNER)
    for part in range(3):
        e[part * SSD_N_HEADS + ch // SSD_HEAD_DIM, ch] = 1.0
    return jnp.asarray(e, dtype=BF16)


def _rope_inv_freq():
    inv = ROPE_THETA ** (-jnp.arange(ROPE_HALF, dtype=F32) * 2.0 / ATTN_HEAD_DIM)
    return jnp.broadcast_to(inv[:, None], (ROPE_HALF, CHUNK))


def _layer(x2, pos_row, batch, norm_mix_pre_w, w_in, ssd_conv_w, ssd_conv_b, ssd_dt_bias, ssd_a_log, ssd_d,
           ssd_norm_w, ssd_w_out, attn_sinks, attn_w_out, w_mix_out, norm_mix_post_w, norm_ffn_pre_w,
           ffn_w_up, ffn_conv_w, ffn_conv_b, ffn_w_down, norm_ffn_post_w, tm):
    o = np.cumsum((0, SSD_D_INNER, SSD_CONV_DIM, SSD_N_HEADS, ATTN_WIDTH, KV_WIDTH, KV_WIDTH, D_MODEL, D_MODEL))
    w_zx = w_in[:, o[0]:o[2]].astype(BF16)
    w_g = w_in[:, o[6]:o[8]].astype(BF16)
    w_dt = jnp.pad(w_in[:, o[2]:o[3]], ((0, 0), (0, DT_PAD - SSD_N_HEADS))).astype(BF16)
    w_qkv = w_in[:, o[3]:o[6]].astype(BF16)
    row = lambda v: v.reshape(1, -1).astype(F32)
    pad_heads = lambda v: jnp.pad(v.astype(F32), (0, DT_PAD - SSD_N_HEADS)).reshape(1, DT_PAD)

    z, xbc, gates, dtp, qkvt = _in_proj(x2, row(norm_mix_pre_w), w_zx, w_g, w_dt, w_qkv, ssd_conv_w.astype(F32),
                                        row(ssd_conv_b), tm, batch)
    yn = _ssd(z, xbc, dtp, pad_heads(ssd_dt_bias), pad_heads(ssd_a_log),
              row(jnp.repeat(ssd_d, SSD_HEAD_DIM)), row(ssd_norm_w), _expansion_matrix(), batch)
    aot = _attn(attn_sinks.astype(F32), qkvt, pos_row, _rope_inv_freq(), batch)
    x1 = _mix_out(yn, aot, gates, x2, ssd_w_out.astype(BF16), attn_w_out.astype(BF16), w_mix_out.astype(BF16),
                  row(norm_mix_post_w), tm)
    return _ffn(x1, row(norm_ffn_pre_w), ffn_w_up.astype(BF16), ffn_conv_w.astype(F32), row(ffn_conv_b),
                ffn_w_down.astype(BF16), row(norm_ffn_post_w), tm, batch)


def kernel(x, positions, norm_mix_pre_w, w_in, ssd_conv_w, ssd_conv_b, ssd_dt_bias, ssd_a_log, ssd_d, ssd_norm_w,
           ssd_w_out, attn_sinks, attn_w_out, w_mix_out, norm_mix_post_w, norm_ffn_pre_w, ffn_w_up, ffn_conv_w,
           ffn_conv_b, ffn_w_down, norm_ffn_post_w):
    batch, seq, d = x.shape
    assert d == D_MODEL and seq % CHUNK == 0
    tm = 512 if seq % 512 == 0 else CHUNK
    x2 = x.reshape(batch * seq, d)
    pos_row = positions.reshape(1, batch * seq)
    for i in range(w_in.shape[0]):
        x2 = _layer(x2, pos_row, batch, norm_mix_pre_w[i], w_in[i], ssd_conv_w[i], ssd_conv_b[i], ssd_dt_bias[i],
                    ssd_a_log[i], ssd_d[i], ssd_norm_w[i], ssd_w_out[i], attn_sinks[i], attn_w_out[i],
                    w_mix_out[i], norm_mix_post_w[i], norm_ffn_pre_w[i], ffn_w_up[i], ffn_conv_w[i],
                    ffn_conv_b[i], ffn_w_down[i], norm_ffn_post_w[i], tm)
    return x2.reshape(batch, seq, d)
```

```python
import functools

import numpy as np
import jax
import jax.numpy as jnp
from jax import lax
from jax.experimental import pallas as pl
from jax.experimental.pallas import tpu as pltpu

F32 = jnp.float32
BF16 = jnp.bfloat16

D_MODEL = 1024
SSD_D_INNER = 2048
SSD_HEAD_DIM = 64
SSD_N_HEADS = 32
SSD_N_GROUPS = 4
SSD_HEADS_PER_GROUP = 8
SSD_D_STATE = 128
SSD_CONV_WIDTH = 4
SSD_CONV_DIM = 3072
SSD_GROUP_WIDTH = SSD_D_INNER // SSD_N_GROUPS
CHUNK = 128

ATTN_HEAD_DIM = 64
ATTN_N_HEADS = 16
ATTN_N_KV_HEADS = 4
ATTN_REP = 4
ATTN_WIDTH = 1024
KV_WIDTH = 256
QKV_WIDTH = ATTN_WIDTH + 2 * KV_WIDTH
ROPE_THETA = 10000.0

FFN_D_FF = 2816
FFN_CONV_WIDTH = 3
NORM_EPS = 1e-6

LANES = 128
SUBLANES = 8
DT_PAD = LANES
GATES_WIDTH = 2 * D_MODEL

VMEM_LIMIT = 56 * 1024 * 1024


def _resident(shape):
    nd = len(shape)
    return pl.BlockSpec(shape, lambda *_: (0,) * nd, pipeline_mode=pl.Buffered(1))


def _sigmoid(v):
    return 1.0 / (1.0 + jnp.exp(-v))


def _split3(v):
    hi = v.astype(BF16).astype(F32)
    r = v - hi
    mid = r.astype(BF16).astype(F32)
    lo = (r - mid).astype(BF16).astype(F32)
    return hi, mid, lo


IN_PROJ_NCHUNK = 512
IN_PROJ_CONV_CHUNK = 256


IN_PROJ_TCHUNK = 768
_NT = (((1,), (1,)), ((), ()))
_TN = (((0,), (0,)), ((), ()))


CONV_HALO = SUBLANES
_LOG2E = float(np.log2(np.e))


def _silu(v):
    return v / (1.0 + jnp.exp2(v * -_LOG2E))


def _causal_conv_rows(r, halo, w, b):
    taps, (rows, width) = w.shape[0], r.shape
    nblk = rows // SUBLANES
    full = jnp.concatenate([halo, r], axis=0).reshape(nblk + 1, SUBLANES, width)
    sub = lax.broadcasted_iota(jnp.int32, (nblk, SUBLANES, width), 1)
    out = b + w[taps - 1:taps, :] * r
    for s in range(1, taps):
        merged = jnp.where(sub >= SUBLANES - s, full[0:nblk], full[1:nblk + 1])
        shifted = pltpu.roll(merged, s, 1).reshape(rows, width)
        out = out + w[taps - 1 - s:taps - s, :] * shifted
    return out


def _in_proj_kernel(x_ref, nw_ref, wzx_ref, wg_ref, wdt_ref, wqkv_ref, cw_ref, cb_ref,
                    z_ref, xbc_ref, gates_ref, dt_ref, qkvt_ref, carry, wt, *, tiles_per_seq):
    tm = x_ref.shape[0]

    @pl.when(pl.program_id(0) == 0)
    def _():
        for c in range(0, QKV_WIDTH, IN_PROJ_TCHUNK):
            wt[c:c + IN_PROJ_TCHUNK, :] = wqkv_ref[:, c:c + IN_PROJ_TCHUNK].astype(F32).T.astype(BF16)

    @pl.when(pl.program_id(0) % tiles_per_seq == 0)
    def _():
        carry[...] = jnp.zeros(carry.shape, F32)

    x = x_ref[...]
    ms = jnp.mean(x * x, axis=-1, keepdims=True)
    u = (x * lax.rsqrt(ms + NORM_EPS) * nw_ref[...]).astype(BF16)
    def plain(out_ref, w_ref, c, step):
        r = jnp.dot(u, w_ref[:, c:c + step], preferred_element_type=F32)
        out_ref[:, c:c + step] = r.astype(out_ref.dtype)

    def transposed(r0):
        r = lax.dot_general(wt[r0:r0 + IN_PROJ_TCHUNK, :], u, _NT, preferred_element_type=F32)
        qkvt_ref[r0:r0 + IN_PROJ_TCHUNK, :] = r.astype(BF16)

    fillers = [functools.partial(plain, out_ref, w_ref, c, min(IN_PROJ_NCHUNK, width))
               for out_ref, w_ref, width in ((z_ref, wzx_ref, SSD_D_INNER), (gates_ref, wg_ref, GATES_WIDTH),
                                             (dt_ref, wdt_ref, DT_PAD))
               for c in range(0, width, min(IN_PROJ_NCHUNK, width))]
    fillers += [functools.partial(transposed, r0) for r0 in range(0, QKV_WIDTH, IN_PROJ_TCHUNK)]
    n_conv = SSD_CONV_DIM // IN_PROJ_CONV_CHUNK
    per_conv = -(-len(fillers) // n_conv)

    for i in range(n_conv):
        c = i * IN_PROJ_CONV_CHUNK
        cs = slice(c, c + IN_PROJ_CONV_CHUNK)
        r = jnp.dot(u, wzx_ref[:, SSD_D_INNER + c:SSD_D_INNER + c + IN_PROJ_CONV_CHUNK],
                    preferred_element_type=F32)
        for f in fillers[i * per_conv:(i + 1) * per_conv]:
            f()
        acc = _causal_conv_rows(r, carry[:, cs], cw_ref[:, cs], cb_ref[:, cs])
        carry[:, cs] = r[tm - CONV_HALO:tm, :]
        xbc_ref[:, cs] = _silu(acc).astype(BF16)


def _in_proj(x2, nw, w_zx, w_g, w_dt, w_qkv, cw, cb, tm, batch):
    t = x2.shape[0]
    row = lambda w: pl.BlockSpec((tm, w), lambda i: (i, 0))
    return pl.pallas_call(
        functools.partial(_in_proj_kernel, tiles_per_seq=t // batch // tm),
        grid=(t // tm,),
        in_specs=[row(D_MODEL), _resident((1, D_MODEL)), _resident((D_MODEL, SSD_D_INNER + SSD_CONV_DIM)),
                  _resident((D_MODEL, GATES_WIDTH)), _resident((D_MODEL, DT_PAD)), _resident((D_MODEL, QKV_WIDTH)),
                  _resident((SSD_CONV_WIDTH, SSD_CONV_DIM)), _resident((1, SSD_CONV_DIM))],
        out_specs=[row(SSD_D_INNER), row(SSD_CONV_DIM), row(GATES_WIDTH), row(DT_PAD),
                   pl.BlockSpec((QKV_WIDTH, tm), lambda i: (0, i))],
        out_shape=[jax.ShapeDtypeStruct((t, SSD_D_INNER), BF16), jax.ShapeDtypeStruct((t, SSD_CONV_DIM), BF16),
                   jax.ShapeDtypeStruct((t, GATES_WIDTH), BF16), jax.ShapeDtypeStruct((t, DT_PAD), F32),
                   jax.ShapeDtypeStruct((QKV_WIDTH, t), BF16)],
        scratch_shapes=[pltpu.VMEM((CONV_HALO, SSD_CONV_DIM), F32), pltpu.VMEM((QKV_WIDTH, D_MODEL), BF16)],
        compiler_params=pltpu.CompilerParams(dimension_semantics=("arbitrary",), vmem_limit_bytes=VMEM_LIMIT),
        name="in_proj",
    )(x2, nw, w_zx, w_g, w_dt, w_qkv, cw, cb)


_COL_B = SSD_D_INNER
_COL_C = SSD_D_INNER + SSD_N_GROUPS * SSD_D_STATE


def _ssd_kernel(z_ref, xbc_ref, dt_ref, dtb_ref, alog_ref, dexp_ref, nw_ref, e3_ref, yn_ref, state, yacc):
    L = CHUNK

    @pl.when(pl.program_id(1) == 0)
    def _():
        state[...] = jnp.zeros(state.shape, F32)

    lane = lax.broadcasted_iota(jnp.int32, (L, LANES), 1)
    head_lane = lane < SSD_N_HEADS
    dtr = dt_ref[...] + dtb_ref[...]
    dt = jnp.maximum(dtr, 0.0) + jnp.log(1.0 + jnp.exp(-jnp.abs(dtr)))
    dt = jnp.where(head_lane, dt, 0.0)
    adt = dt * (-jnp.exp(alog_ref[...]))
    row_i = lax.broadcasted_iota(jnp.int32, (L, L), 0)
    col_i = lax.broadcasted_iota(jnp.int32, (L, L), 1)
    causal = col_i <= row_i
    tril = jnp.where(causal, 1.0, 0.0).astype(BF16)
    acs = sum(jnp.dot(tril, p.astype(BF16), preferred_element_type=F32) for p in _split3(adt)) * _LOG2E
    acs_dt_t = (acs - jnp.log2(dt)).T

    def pack3(v):
        hi, mid, lo = _split3(v)
        return jnp.where(head_lane, hi, jnp.where(lane < 2 * SSD_N_HEADS, pltpu.roll(mid, SSD_N_HEADS, 1),
                                                  pltpu.roll(lo, 2 * SSD_N_HEADS, 1))).astype(BF16)

    e_h = jnp.exp2(acs)
    w_h = dt * jnp.exp2(acs[L - 1:L, :] - acs)
    exp_in = jnp.concatenate([pack3(e_h), pack3(w_h)], axis=0)
    expd = jnp.dot(exp_in, e3_ref[...], preferred_element_type=F32)
    e_x = expd[0:L, :]
    w_x = expd[L:2 * L, :]
    cd_x = e_x[L - 1:L, :]

    for g in range(SSD_N_GROUPS):
        gs = slice(g * SSD_GROUP_WIDTH, (g + 1) * SSD_GROUP_WIDTH)
        xb = xbc_ref[:, gs]
        xg = xb.astype(F32)
        bm = xbc_ref[:, _COL_B + g * SSD_D_STATE:_COL_B + (g + 1) * SSD_D_STATE]
        cm = xbc_ref[:, _COL_C + g * SSD_D_STATE:_COL_C + (g + 1) * SSD_D_STATE]
        st = state[g]
        yoff = jnp.dot(cm, st.astype(BF16), preferred_element_type=F32) * e_x[:, gs]
        xd = (xg * w_x[:, gs]).astype(BF16)
        upd = lax.dot_general(bm, xd, _TN, preferred_element_type=F32)
        state[g] = st * cd_x[:, gs] + upd
        yacc[:, gs] = yoff + dexp_ref[:, gs] * xg
        cbm = lax.dot_general(cm, bm, _NT, preferred_element_type=F32).astype(BF16)
        for jp in range(0, SSD_HEADS_PER_GROUP, 2):
            pair = []
            for j in (jp, jp + 1):
                h = g * SSD_HEADS_PER_GROUP + j
                seg = acs[:, h:h + 1] - acs_dt_t[h:h + 1, :]
                m = jnp.where(causal, jnp.exp2(seg), 0.0).astype(BF16) * cbm
                pair.append(jnp.dot(m, xb[:, j * SSD_HEAD_DIM:(j + 1) * SSD_HEAD_DIM],
                                    preferred_element_type=F32))
            c0 = g * SSD_GROUP_WIDTH + jp * SSD_HEAD_DIM
            yacc[:, c0:c0 + LANES] += jnp.concatenate(pair, axis=1)

    for g in range(SSD_N_GROUPS):
        gs = slice(g * SSD_GROUP_WIDTH, (g + 1) * SSD_GROUP_WIDTH)
        gv = yacc[:, gs] * _silu(z_ref[:, gs].astype(F32))
        ms = jnp.mean(gv * gv, axis=-1, keepdims=True)
        yn_ref[:, gs] = (gv * lax.rsqrt(ms + NORM_EPS) * nw_ref[:, gs]).astype(BF16)


def _ssd(z, xbc, dtp, dtb, alog, dexp, nw, e3, batch):
    t = z.shape[0]
    nc = t // batch // CHUNK
    row = lambda w: pl.BlockSpec((CHUNK, w), lambda b, c: (b * nc + c, 0))
    return pl.pallas_call(
        _ssd_kernel,
        grid=(batch, nc),
        in_specs=[row(SSD_D_INNER), row(SSD_CONV_DIM), row(DT_PAD),
                  _resident((1, DT_PAD)), _resident((1, DT_PAD)), _resident((1, SSD_D_INNER)),
                  _resident((1, SSD_D_INNER)), _resident((LANES, SSD_D_INNER))],
        out_specs=row(SSD_D_INNER),
        out_shape=jax.ShapeDtypeStruct((t, SSD_D_INNER), BF16),
        scratch_shapes=[pltpu.VMEM((SSD_N_GROUPS, SSD_D_STATE, SSD_GROUP_WIDTH), F32),
                        pltpu.VMEM((CHUNK, SSD_D_INNER), F32)],
        compiler_params=pltpu.CompilerParams(dimension_semantics=("arbitrary", "arbitrary"),
                                             vmem_limit_bytes=VMEM_LIMIT),
        name="ssd",
    )(z, xbc, dtp, dtb, alog, dexp, nw, e3)


ROPE_HALF = ATTN_HEAD_DIM // 2


def _attn_kernel(sink_ref, qkvt_ref, pos_ref, invf_ref, aot_ref, qt, kk, vvt):
    L = CHUNK
    blk = pl.program_id(1)

    @pl.when(blk == 0)
    def _():
        kk[:, L:2 * L, :] = jnp.zeros((ATTN_N_KV_HEADS, L, ATTN_HEAD_DIM), BF16)
        vvt[:, L:2 * L] = jnp.zeros((KV_WIDTH, L), BF16)

    @pl.when(blk > 0)
    def _():
        kk[:, L:2 * L, :] = kk[:, 0:L, :]
        vvt[:, L:2 * L] = vvt[:, 0:L]

    ang = invf_ref[...] * pos_ref[...].astype(F32)
    cosv = jnp.cos(ang)
    sinv = jnp.sin(ang)

    def rope(r0):
        t1 = qkvt_ref[r0:r0 + ROPE_HALF, :].astype(F32)
        t2 = qkvt_ref[r0 + ROPE_HALF:r0 + ATTN_HEAD_DIM, :].astype(F32)
        return jnp.concatenate([t1 * cosv - t2 * sinv, t2 * cosv + t1 * sinv], axis=0)

    scale = ATTN_HEAD_DIM ** -0.5 * _LOG2E
    for h in range(ATTN_N_HEADS):
        r0 = h * ATTN_HEAD_DIM
        qt[r0:r0 + ATTN_HEAD_DIM, :] = (rope(r0) * scale).astype(BF16)
    for g in range(ATTN_N_KV_HEADS):
        kk[g, 0:L, :] = rope(ATTN_WIDTH + g * ATTN_HEAD_DIM).T.astype(BF16)
    vvt[:, 0:L] = qkvt_ref[ATTN_WIDTH + KV_WIDTH:QKV_WIDTH, :]

    cols = ATTN_REP * L
    key_i = lax.broadcasted_iota(jnp.int32, (L, cols), 0)
    qry_i = lax.broadcasted_iota(jnp.int32, (L, cols), 1) % L
    in_cur = key_i <= qry_i
    old_bias = jnp.where(blk > 0, 0.0, -jnp.inf)
    groups = range(ATTN_N_KV_HEADS)
    heads = [range(g * ATTN_REP, (g + 1) * ATTN_REP) for g in groups]
    s2 = [jnp.dot(kk[g], jnp.concatenate([qt[h * ATTN_HEAD_DIM:(h + 1) * ATTN_HEAD_DIM, :] for h in heads[g]],
                                         axis=1), preferred_element_type=F32) for g in groups]
    s = [jnp.where(in_cur, s2[g][0:L], s2[g][L:2 * L] + old_bias) for g in groups]
    sink = [jnp.concatenate([jnp.full((1, L), sink_ref[h] * _LOG2E, F32) for h in heads[g]], axis=1) for g in groups]
    m = [jnp.maximum(jnp.max(s[g], axis=0, keepdims=True), sink[g]) for g in groups]
    p = [jnp.exp2(s[g] - m[g]) for g in groups]
    denom = [jnp.sum(p[g], axis=0, keepdims=True) + jnp.exp2(sink[g] - m[g]) for g in groups]
    pt = [jnp.concatenate([jnp.where(in_cur, p[g], 0.0), jnp.where(in_cur, 0.0, p[g])], axis=0).astype(BF16)
          for g in groups]
    o = [jnp.dot(vvt[g * ATTN_HEAD_DIM:(g + 1) * ATTN_HEAD_DIM, :], pt[g], preferred_element_type=F32)
         * (1.0 / denom[g]) for g in groups]
    for g in groups:
        for r, h in enumerate(heads[g]):
            aot_ref[h * ATTN_HEAD_DIM:(h + 1) * ATTN_HEAD_DIM, :] = o[g][:, r * L:(r + 1) * L].astype(BF16)


def _attn(sinks, qkvt, pos_row, invf, batch):
    t = qkvt.shape[1]
    nb = t // batch // CHUNK
    col = lambda h: pl.BlockSpec((h, CHUNK), lambda b, c: (0, b * nb + c))
    return pl.pallas_call(
        _attn_kernel,
        grid=(batch, nb),
        in_specs=[pl.BlockSpec(memory_space=pltpu.SMEM), col(QKV_WIDTH), col(1), _resident((ROPE_HALF, CHUNK))],
        out_specs=col(ATTN_WIDTH),
        out_shape=jax.ShapeDtypeStruct((ATTN_WIDTH, t), BF16),
        scratch_shapes=[pltpu.VMEM((ATTN_WIDTH, CHUNK), BF16),
                        pltpu.VMEM((ATTN_N_KV_HEADS, 2 * CHUNK, ATTN_HEAD_DIM), BF16),
                        pltpu.VMEM((KV_WIDTH, 2 * CHUNK), BF16)],
        compiler_params=pltpu.CompilerParams(dimension_semantics=("arbitrary", "arbitrary"),
                                             vmem_limit_bytes=VMEM_LIMIT),
        name="attn",
    )(sinks, qkvt, pos_row, invf)


MIX_NCHUNK = 256


def _attn_mix_kernel(sink_ref, qkvt_ref, pos_ref, invf_ref, yn_ref, gates_ref, x_ref, wso_ref, wao_ref, wmix_ref,
                     nw_ref, o_ref, qt, kk, vvt, ao_s, *, tiles_per_seq, n_tiles):
    L = CHUNK
    j = pl.program_id(0)
    tm = x_ref.shape[0]
    first_tile = (jnp.minimum(j, n_tiles - 1) % tiles_per_seq) == 0

    @pl.when(j == 0)
    def _():
        ao_s[...] = jnp.zeros(ao_s.shape, BF16)
        kk[...] = jnp.zeros(kk.shape, BF16)
        vvt[...] = jnp.zeros(vvt.shape, BF16)

    slot = j % 2
    cols = ATTN_REP * L
    key_i = lax.broadcasted_iota(jnp.int32, (L, cols), 0)
    qry_i = lax.broadcasted_iota(jnp.int32, (L, cols), 1) % L
    in_cur = key_i <= qry_i
    scale = ATTN_HEAD_DIM ** -0.5 * _LOG2E
    groups = range(ATTN_N_KV_HEADS)
    heads = [range(g * ATTN_REP, (g + 1) * ATTN_REP) for g in groups]
    sink = [jnp.concatenate([jnp.full((1, L), sink_ref[h] * _LOG2E, F32) for h in heads[g]], axis=1) for g in groups]

    def attn_stages(b):
        ts = slice(b * L, (b + 1) * L)
        pc, po = b % 2, 1 - b % 2
        ang = invf_ref[...] * pos_ref[:, ts].astype(F32)
        cosv = jnp.cos(ang)
        sinv = jnp.sin(ang)

        def rope(r0):
            t1 = qkvt_ref[r0:r0 + ROPE_HALF, ts].astype(F32)
            t2 = qkvt_ref[r0 + ROPE_HALF:r0 + ATTN_HEAD_DIM, ts].astype(F32)
            return jnp.concatenate([t1 * cosv - t2 * sinv, t2 * cosv + t1 * sinv], axis=0)

        for h in range(ATTN_N_HEADS):
            r0 = h * ATTN_HEAD_DIM
            qt[r0:r0 + ATTN_HEAD_DIM, :] = (rope(r0) * scale).astype(BF16)
            if h % 8 == 7:
                yield
        for g in groups:
            kk[pc, g] = rope(ATTN_WIDTH + g * ATTN_HEAD_DIM).T.astype(BF16)
        vvt[pc] = qkvt_ref[ATTN_WIDTH + KV_WIDTH:QKV_WIDTH, ts]
        yield
        old_bias = jnp.where(first_tile, -jnp.inf, 0.0) if b == 0 else 0.0
        s2 = [jnp.dot(jnp.concatenate([kk[pc, g], kk[po, g]], axis=0),
                      jnp.concatenate([qt[h * ATTN_HEAD_DIM:(h + 1) * ATTN_HEAD_DIM, :] for h in heads[g]], axis=1),
                      preferred_element_type=F32) for g in groups]
        yield
        s = [jnp.where(in_cur, s2[g][0:L], s2[g][L:2 * L] + old_bias) for g in groups]
        m = [jnp.maximum(jnp.max(s[g], axis=0, keepdims=True), sink[g]) for g in groups]
        yield
        p = [jnp.exp2(s[g] - m[g]) for g in groups]
        denom = [jnp.sum(p[g], axis=0, keepdims=True) + jnp.exp2(sink[g] - m[g]) for g in groups]
        yield
        pt = [jnp.concatenate([jnp.where(in_cur, p[g], 0.0), jnp.where(in_cur, 0.0, p[g])], axis=0).astype(BF16)
              for g in groups]
        yield
        vs = [slice(g * ATTN_HEAD_DIM, (g + 1) * ATTN_HEAD_DIM) for g in groups]
        o = [jnp.dot(jnp.concatenate([vvt[pc, vs[g], :], vvt[po, vs[g], :]], axis=1), pt[g],
                     preferred_element_type=F32) * (1.0 / denom[g]) for g in groups]
        for g in groups:
            for r, h in enumerate(heads[g]):
                ao_s[slot, b, h * ATTN_HEAD_DIM:(h + 1) * ATTN_HEAD_DIM, :] = o[g][:, r * L:(r + 1) * L].astype(BF16)
        yield

    def mix_stages():
        nc = D_MODEL // MIX_NCHUNK
        csl = [slice(c * MIX_NCHUNK, (c + 1) * MIX_NCHUNK) for c in range(nc)]
        yn = yn_ref[...]
        ys = []
        for c in range(nc):
            ys.append(jnp.dot(yn, wso_ref[:, csl[c]], preferred_element_type=F32))
            yield
        ya = jnp.concatenate([lax.dot_general(ao_s[1 - slot, b], wao_ref[...], _TN, preferred_element_type=F32)
                              for b in range(tm // L)], axis=0)
        yield
        merged = []
        for c in range(nc):
            gs = _sigmoid(gates_ref[:, csl[c]].astype(F32))
            ga = _sigmoid(gates_ref[:, D_MODEL + c * MIX_NCHUNK:D_MODEL + (c + 1) * MIX_NCHUNK].astype(F32))
            merged.append((gs * ys[c] + ga * ya[:, csl[c]]).astype(BF16))
        merged = jnp.concatenate(merged, axis=1)
        yield
        mo = []
        for c in range(nc):
            mo.append(jnp.dot(merged, wmix_ref[:, csl[c]], preferred_element_type=F32))
            yield
        mo = jnp.concatenate(mo, axis=1)
        ms = jnp.mean(mo * mo, axis=-1, keepdims=True)
        o_ref[...] = x_ref[...] + mo * lax.rsqrt(ms + NORM_EPS) * nw_ref[...]
        yield

    att = (st for b in range(tm // L) for st in attn_stages(b))
    mix = mix_stages()
    n_att, n_mix = (tm // L) * 8, 2 * (D_MODEL // MIX_NCHUNK) + 3
    done_att = 0
    for i in range(n_mix):
        next(mix)
        want = (i + 1) * n_att // n_mix
        while done_att < want:
            next(att)
            done_att += 1
    assert next(att, "end") == "end" and next(mix, "end") == "end"


def _attn_mix(sinks, qkvt, pos_row, invf, yn, gates, x2, wso, wao, wmix, nw, tm, batch):
    t = x2.shape[0]
    n_tiles = t // tm
    assert tm == ATTN_REP * CHUNK
    cur = lambda h: pl.BlockSpec((h, tm), lambda j: (0, jnp.minimum(j, n_tiles - 1)))
    prev = lambda w: pl.BlockSpec((tm, w), lambda j: (jnp.maximum(j - 1, 0), 0))
    return pl.pallas_call(
        functools.partial(_attn_mix_kernel, tiles_per_seq=t // batch // tm, n_tiles=n_tiles),
        grid=(n_tiles + 1,),
        in_specs=[pl.BlockSpec(memory_space=pltpu.SMEM), cur(QKV_WIDTH), cur(1), _resident((ROPE_HALF, CHUNK)),
                  prev(SSD_D_INNER), prev(GATES_WIDTH), prev(D_MODEL),
                  _resident((SSD_D_INNER, D_MODEL)), _resident((ATTN_WIDTH, D_MODEL)),
                  _resident((D_MODEL, D_MODEL)), _resident((1, D_MODEL))],
        out_specs=prev(D_MODEL),
        out_shape=jax.ShapeDtypeStruct((t, D_MODEL), F32),
        scratch_shapes=[pltpu.VMEM((ATTN_WIDTH, CHUNK), BF16),
                        pltpu.VMEM((2, ATTN_N_KV_HEADS, CHUNK, ATTN_HEAD_DIM), BF16),
                        pltpu.VMEM((2, KV_WIDTH, CHUNK), BF16),
                        pltpu.VMEM((2, tm // CHUNK, ATTN_WIDTH, CHUNK), BF16)],
        compiler_params=pltpu.CompilerParams(dimension_semantics=("arbitrary",), vmem_limit_bytes=VMEM_LIMIT),
        name="attn_mix",
    )(sinks, qkvt, pos_row, invf, yn, gates, x2, wso, wao, wmix, nw)


def _mix_out_kernel(yn_ref, aot_ref, gates_ref, x_ref, wso_ref, wao_ref, wmix_ref, nw_ref, o_ref):
    ys = jnp.dot(yn_ref[...], wso_ref[...], preferred_element_type=F32)
    ya = lax.dot_general(aot_ref[...], wao_ref[...], _TN, preferred_element_type=F32)
    gs = _sigmoid(gates_ref[:, 0:D_MODEL].astype(F32))
    ga = _sigmoid(gates_ref[:, D_MODEL:GATES_WIDTH].astype(F32))
    merged = (gs * ys + ga * ya).astype(BF16)
    mo = jnp.dot(merged, wmix_ref[...], preferred_element_type=F32)
    ms = jnp.mean(mo * mo, axis=-1, keepdims=True)
    o_ref[...] = x_ref[...] + mo * lax.rsqrt(ms + NORM_EPS) * nw_ref[...]


def _mix_out(yn, ao, gates, x2, wso, wao, wmix, nw, tm):
    t = x2.shape[0]
    row = lambda w: pl.BlockSpec((tm, w), lambda i: (i, 0))
    return pl.pallas_call(
        _mix_out_kernel,
        grid=(t // tm,),
        in_specs=[row(SSD_D_INNER), pl.BlockSpec((ATTN_WIDTH, tm), lambda i: (0, i)), row(GATES_WIDTH), row(D_MODEL),
                  _resident((SSD_D_INNER, D_MODEL)), _resident((ATTN_WIDTH, D_MODEL)),
                  _resident((D_MODEL, D_MODEL)), _resident((1, D_MODEL))],
        out_specs=row(D_MODEL),
        out_shape=jax.ShapeDtypeStruct((t, D_MODEL), F32),
        compiler_params=pltpu.CompilerParams(dimension_semantics=("arbitrary",), vmem_limit_bytes=VMEM_LIMIT),
        name="mix_out",
    )(yn, ao, gates, x2, wso, wao, wmix, nw)


FFN_NCHUNK = 256
FFN_SUBTILE = 256
_GELU_A = float(-2.0 * np.sqrt(2.0 / np.pi) * np.log2(np.e))
_GELU_B = float(_GELU_A * 0.044715)


def _gelu_tanh(v):
    return v / (1.0 + jnp.exp2(v * (_GELU_A + _GELU_B * (v * v))))


def _ffn_kernel(x_ref, npre_ref, wup_ref, cw_ref, cb_ref, wdn_ref, npost_ref, o_ref, carry, act,
                *, tiles_per_seq):
    tm = x_ref.shape[0]
    ts = min(FFN_SUBTILE, tm)

    @pl.when(pl.program_id(0) % tiles_per_seq == 0)
    def _():
        carry[...] = jnp.zeros(carry.shape, F32)

    def up(h, c0):
        return (jnp.dot(h, wup_ref[:, c0:c0 + FFN_NCHUNK], preferred_element_type=F32),
                jnp.dot(h, wup_ref[:, FFN_D_FF + c0:FFN_D_FF + c0 + FFN_NCHUNK], preferred_element_type=F32))

    def conv_chunk(r, c0):
        cs = slice(c0, c0 + FFN_NCHUNK)
        out = _causal_conv_rows(r, carry[:, cs], cw_ref[:, cs], cb_ref[:, cs])
        carry[:, cs] = r[ts - CONV_HALO:ts, :]
        return out

    chunks = list(range(0, FFN_D_FF, FFN_NCHUNK))
    for r0 in range(0, tm, ts):
        x = x_ref[r0:r0 + ts, :]
        ms = jnp.mean(x * x, axis=-1, keepdims=True)
        h = (x * lax.rsqrt(ms + NORM_EPS) * npre_ref[...]).astype(BF16)
        nxt = up(h, chunks[0])
        for i, c0 in enumerate(chunks):
            cur = nxt
            if i + 1 < len(chunks):
                nxt = up(h, chunks[i + 1])
            gate = conv_chunk(cur[0], c0).astype(BF16)
            val = conv_chunk(cur[1], FFN_D_FF + c0).astype(BF16)
            act[r0:r0 + ts, c0:c0 + FFN_NCHUNK] = _gelu_tanh(gate) * val
        ff = jnp.dot(act[r0:r0 + ts, :], wdn_ref[...], preferred_element_type=F32)
        ms2 = jnp.mean(ff * ff, axis=-1, keepdims=True)
        o_ref[r0:r0 + ts, :] = x + ff * lax.rsqrt(ms2 + NORM_EPS) * npost_ref[...]


def _ffn(x1, npre, wup, cw, cb, wdn, npost, tm, batch):
    t = x1.shape[0]
    row = lambda w: pl.BlockSpec((tm, w), lambda i: (i, 0))
    return pl.pallas_call(
        functools.partial(_ffn_kernel, tiles_per_seq=t // batch // tm),
        grid=(t // tm,),
        in_specs=[row(D_MODEL), _resident((1, D_MODEL)), _resident((D_MODEL, 2 * FFN_D_FF)),
                  _resident((FFN_CONV_WIDTH, 2 * FFN_D_FF)), _resident((1, 2 * FFN_D_FF)),
                  _resident((FFN_D_FF, D_MODEL)), _resident((1, D_MODEL))],
        out_specs=row(D_MODEL),
        out_shape=jax.ShapeDtypeStruct((t, D_MODEL), F32),
        scratch_shapes=[pltpu.VMEM((CONV_HALO, 2 * FFN_D_FF), F32),
                        pltpu.VMEM((tm, FFN_D_FF), BF16)],
        compiler_params=pltpu.CompilerParams(dimension_semantics=("arbitrary",), vmem_limit_bytes=VMEM_LIMIT),
        name="ffn",
    )(x1, npre, wup, cw, cb, wdn, npost)


def _expansion_matrix():
    e = np.zeros((LANES, SSD_D_INNER), np.float32)
    ch = np.arange(SSD_D_INNER)
    for part in range(3):
        e[part * SSD_N_HEADS + ch // SSD_HEAD_DIM, ch] = 1.0
    return jnp.asarray(e, dtype=BF16)


def _rope_inv_freq():
    inv = ROPE_THETA ** (-jnp.arange(ROPE_HALF, dtype=F32) * 2.0 / ATTN_HEAD_DIM)
    return jnp.broadcast_to(inv[:, None], (ROPE_HALF, CHUNK))


def _layer(x2, pos_row, batch, norm_mix_pre_w, w_in, ssd_conv_w, ssd_conv_b, ssd_dt_bias, ssd_a_log, ssd_d,
           ssd_norm_w, ssd_w_out, attn_sinks, attn_w_out, w_mix_out, norm_mix_post_w, norm_ffn_pre_w,
           ffn_w_up, ffn_conv_w, ffn_conv_b, ffn_w_down, norm_ffn_post_w, tm):
    o = np.cumsum((0, SSD_D_INNER, SSD_CONV_DIM, SSD_N_HEADS, ATTN_WIDTH, KV_WIDTH, KV_WIDTH, D_MODEL, D_MODEL))
    w_zx = w_in[:, o[0]:o[2]].astype(BF16)
    w_g = w_in[:, o[6]:o[8]].astype(BF16)
    w_dt = jnp.pad(w_in[:, o[2]:o[3]], ((0, 0), (0, DT_PAD - SSD_N_HEADS))).astype(BF16)
    w_qkv = w_in[:, o[3]:o[6]].astype(BF16)
    row = lambda v: v.reshape(1, -1).astype(F32)
    pad_heads = lambda v: jnp.pad(v.astype(F32), (0, DT_PAD - SSD_N_HEADS)).reshape(1, DT_PAD)

    z, xbc, gates, dtp, qkvt = _in_proj(x2, row(norm_mix_pre_w), w_zx, w_g, w_dt, w_qkv, ssd_conv_w.astype(F32),
                                        row(ssd_conv_b), tm, batch)
    yn = _ssd(z, xbc, dtp, pad_heads(ssd_dt_bias), pad_heads(ssd_a_log),
              row(jnp.repeat(ssd_d, SSD_HEAD_DIM)), row(ssd_norm_w), _expansion_matrix(), batch)
    x1 = _attn_mix(attn_sinks.astype(F32), qkvt, pos_row, _rope_inv_freq(), yn, gates, x2, ssd_w_out.astype(BF16),
                   attn_w_out.astype(BF16), w_mix_out.astype(BF16), row(norm_mix_post_w), tm, batch)
    return _ffn(x1, row(norm_ffn_pre_w), ffn_w_up.astype(BF16), ffn_conv_w.astype(F32), row(ffn_conv_b),
                ffn_w_down.astype(BF16), row(norm_ffn_post_w), tm, batch)


def kernel(x, positions, norm_mix_pre_w, w_in, ssd_conv_w, ssd_conv_b, ssd_dt_bias, ssd_a_log, ssd_d, ssd_norm_w,
           ssd_w_out, attn_sinks, attn_w_out, w_mix_out, norm_mix_post_w, norm_ffn_pre_w, ffn_w_up, ffn_conv_w,
           ffn_conv_b, ffn_w_down, norm_ffn_post_w):
    batch, seq, d = x.shape
    assert d == D_MODEL and seq % CHUNK == 0
    tm = 512 if seq % 512 == 0 else CHUNK
    x2 = x.reshape(batch * seq, d)
    pos_row = positions.reshape(1, batch * seq)
    for i in range(w_in.shape[0]):
        x2 = _layer(x2, pos_row, batch, norm_mix_pre_w[i], w_in[i], ssd_conv_w[i], ssd_conv_b[i], ssd_dt_bias[i],
                    ssd_a_log[i], ssd_d[i], ssd_norm_w[i], ssd_w_out[i], attn_sinks[i], attn_w_out[i],
                    w_mix_out[i], norm_mix_post_w[i], norm_ffn_pre_w[i], ffn_w_up[i], ffn_conv_w[i],
                    ffn_conv_b[i], ffn_w_down[i], norm_ffn_post_w[i], tm)
    return x2.reshape(batch, seq, d)
```

```python
import functools

import numpy as np
import jax
import jax.numpy as jnp
from jax import lax
from jax.experimental import pallas as pl
from jax.experimental.pallas import tpu as pltpu

F32 = jnp.float32
BF16 = jnp.bfloat16

D_MODEL = 1024
SSD_D_INNER = 2048
SSD_HEAD_DIM = 64
SSD_N_HEADS = 32
SSD_N_GROUPS = 4
SSD_HEADS_PER_GROUP = 8
SSD_D_STATE = 128
SSD_CONV_WIDTH = 4
SSD_CONV_DIM = 3072
SSD_GROUP_WIDTH = SSD_D_INNER // SSD_N_GROUPS
CHUNK = 128

ATTN_HEAD_DIM = 64
ATTN_N_HEADS = 16
ATTN_N_KV_HEADS = 4
ATTN_REP = 4
ATTN_WIDTH = 1024
KV_WIDTH = 256
QKV_WIDTH = ATTN_WIDTH + 2 * KV_WIDTH
ROPE_THETA = 10000.0

FFN_D_FF = 2816
FFN_CONV_WIDTH = 3
NORM_EPS = 1e-6

LANES = 128
SUBLANES = 8
DT_PAD = LANES
GATES_WIDTH = 2 * D_MODEL

VMEM_V7X = 64 * 1024 * 1024
VMEM_LIMIT = 56 * 1024 * 1024


def _resident(shape):
    nd = len(shape)
    return pl.BlockSpec(shape, lambda *_: (0,) * nd, pipeline_mode=pl.Buffered(1))


def _sigmoid(v):
    return 1.0 / (1.0 + jnp.exp(-v))


def _split3(v):
    hi = v.astype(BF16).astype(F32)
    r = v - hi
    mid = r.astype(BF16).astype(F32)
    lo = (r - mid).astype(BF16).astype(F32)
    return hi, mid, lo


IN_PROJ_NCHUNK = 256
IN_PROJ_CONV_CHUNK = 256


IN_PROJ_TCHUNK = 256
_NT = (((1,), (1,)), ((), ()))
_TN = (((0,), (0,)), ((), ()))


CONV_HALO = SUBLANES
_LOG2E = float(np.log2(np.e))


def _silu(v):
    return v / (1.0 + jnp.exp2(v * -_LOG2E))


def _causal_conv_rows(r, halo, w, b):
    taps, (rows, width) = w.shape[0], r.shape
    nblk = rows // SUBLANES
    full = jnp.concatenate([halo, r], axis=0).reshape(nblk + 1, SUBLANES, width)
    sub = lax.broadcasted_iota(jnp.int32, (nblk, SUBLANES, width), 1)
    out = b + w[taps - 1:taps, :] * r
    for s in range(1, taps):
        merged = jnp.where(sub >= SUBLANES - s, full[0:nblk], full[1:nblk + 1])
        shifted = pltpu.roll(merged, s, 1).reshape(rows, width)
        out = out + w[taps - 1 - s:taps - s, :] * shifted
    return out


def _in_proj_kernel(x_ref, nw_ref, wzx_ref, wg_ref, wdt_ref, wqkv_ref, cw_ref, cb_ref,
                    z_ref, xbc_ref, gates_ref, dt_ref, qkvt_ref, carry, wt, *, tiles_per_seq):
    tm = x_ref.shape[0]

    @pl.when(pl.program_id(0) == 0)
    def _():
        for c in range(0, QKV_WIDTH, IN_PROJ_TCHUNK):
            wt[c:c + IN_PROJ_TCHUNK, :] = wqkv_ref[:, c:c + IN_PROJ_TCHUNK].astype(F32).T.astype(BF16)

    @pl.when(pl.program_id(0) % tiles_per_seq == 0)
    def _():
        carry[...] = jnp.zeros(carry.shape, F32)

    x = x_ref[...]
    ms = jnp.mean(x * x, axis=-1, keepdims=True)
    u = (x * lax.rsqrt(ms + NORM_EPS) * nw_ref[...]).astype(BF16)
    def plain(out_ref, w_ref, c, step):
        r = jnp.dot(u, w_ref[:, c:c + step], preferred_element_type=F32)
        out_ref[:, c:c + step] = r.astype(out_ref.dtype)

    def transposed(r0):
        r = lax.dot_general(wt[r0:r0 + IN_PROJ_TCHUNK, :], u, _NT, preferred_element_type=F32)
        qkvt_ref[r0:r0 + IN_PROJ_TCHUNK, :] = r.astype(BF16)

    fillers = [functools.partial(plain, out_ref, w_ref, c, min(IN_PROJ_NCHUNK, width))
               for out_ref, w_ref, width in ((z_ref, wzx_ref, SSD_D_INNER), (gates_ref, wg_ref, GATES_WIDTH),
                                             (dt_ref, wdt_ref, DT_PAD))
               for c in range(0, width, min(IN_PROJ_NCHUNK, width))]
    fillers += [functools.partial(transposed, r0) for r0 in range(0, QKV_WIDTH, IN_PROJ_TCHUNK)]
    n_conv = SSD_CONV_DIM // IN_PROJ_CONV_CHUNK
    per_conv = -(-len(fillers) // n_conv)

    for i in range(n_conv):
        c = i * IN_PROJ_CONV_CHUNK
        cs = slice(c, c + IN_PROJ_CONV_CHUNK)
        r = jnp.dot(u, wzx_ref[:, SSD_D_INNER + c:SSD_D_INNER + c + IN_PROJ_CONV_CHUNK],
                    preferred_element_type=F32)
        for f in fillers[i * per_conv:(i + 1) * per_conv]:
            f()
        acc = _causal_conv_rows(r, carry[:, cs], cw_ref[:, cs], cb_ref[:, cs])
        carry[:, cs] = r[tm - CONV_HALO:tm, :]
        xbc_ref[:, cs] = _silu(acc).astype(BF16)


def _in_proj(x2, nw, w_zx, w_g, w_dt, w_qkv, cw, cb, tm, batch):
    t = x2.shape[0]
    row = lambda w: pl.BlockSpec((tm, w), lambda i: (i, 0))
    return pl.pallas_call(
        functools.partial(_in_proj_kernel, tiles_per_seq=t // batch // tm),
        grid=(t // tm,),
        in_specs=[row(D_MODEL), _resident((1, D_MODEL)), _resident((D_MODEL, SSD_D_INNER + SSD_CONV_DIM)),
                  _resident((D_MODEL, GATES_WIDTH)), _resident((D_MODEL, DT_PAD)), _resident((D_MODEL, QKV_WIDTH)),
                  _resident((SSD_CONV_WIDTH, SSD_CONV_DIM)), _resident((1, SSD_CONV_DIM))],
        out_specs=[row(SSD_D_INNER), row(SSD_CONV_DIM), row(GATES_WIDTH), row(DT_PAD),
                   pl.BlockSpec((QKV_WIDTH, tm), lambda i: (0, i))],
        out_shape=[jax.ShapeDtypeStruct((t, SSD_D_INNER), BF16), jax.ShapeDtypeStruct((t, SSD_CONV_DIM), BF16),
                   jax.ShapeDtypeStruct((t, GATES_WIDTH), BF16), jax.ShapeDtypeStruct((t, DT_PAD), F32),
                   jax.ShapeDtypeStruct((QKV_WIDTH, t), BF16)],
        scratch_shapes=[pltpu.VMEM((CONV_HALO, SSD_CONV_DIM), F32), pltpu.VMEM((QKV_WIDTH, D_MODEL), BF16)],
        compiler_params=pltpu.CompilerParams(dimension_semantics=("arbitrary",), vmem_limit_bytes=VMEM_LIMIT),
        name="in_proj",
    )(x2, nw, w_zx, w_g, w_dt, w_qkv, cw, cb)


_COL_B = SSD_D_INNER
_COL_C = SSD_D_INNER + SSD_N_GROUPS * SSD_D_STATE


def _ssd_kernel(z_ref, xbc_ref, dt_ref, dtb_ref, alog_ref, dexp_ref, nw_ref, e3_ref, yn_ref, state, yacc):
    L = CHUNK

    @pl.when(pl.program_id(1) == 0)
    def _():
        state[...] = jnp.zeros(state.shape, F32)

    lane = lax.broadcasted_iota(jnp.int32, (L, LANES), 1)
    head_lane = lane < SSD_N_HEADS
    dtr = dt_ref[...] + dtb_ref[...]
    dt = jnp.maximum(dtr, 0.0) + jnp.log(1.0 + jnp.exp(-jnp.abs(dtr)))
    dt = jnp.where(head_lane, dt, 0.0)
    adt = dt * (-jnp.exp(alog_ref[...]))
    row_i = lax.broadcasted_iota(jnp.int32, (L, L), 0)
    col_i = lax.broadcasted_iota(jnp.int32, (L, L), 1)
    causal = col_i <= row_i
    tril = jnp.where(causal, 1.0, 0.0).astype(BF16)
    acs = sum(jnp.dot(tril, p.astype(BF16), preferred_element_type=F32) for p in _split3(adt)) * _LOG2E
    acs_dt_t = (acs - jnp.log2(dt)).T

    def pack3(v):
        hi, mid, lo = _split3(v)
        return jnp.where(head_lane, hi, jnp.where(lane < 2 * SSD_N_HEADS, pltpu.roll(mid, SSD_N_HEADS, 1),
                                                  pltpu.roll(lo, 2 * SSD_N_HEADS, 1))).astype(BF16)

    e_h = jnp.exp2(acs)
    w_h = dt * jnp.exp2(acs[L - 1:L, :] - acs)
    exp_in = jnp.concatenate([pack3(e_h), pack3(w_h)], axis=0)
    expd = jnp.dot(exp_in, e3_ref[...], preferred_element_type=F32)
    e_x = expd[0:L, :]
    w_x = expd[L:2 * L, :]
    cd_x = e_x[L - 1:L, :]

    for g in range(SSD_N_GROUPS):
        gs = slice(g * SSD_GROUP_WIDTH, (g + 1) * SSD_GROUP_WIDTH)
        xb = xbc_ref[:, gs]
        xg = xb.astype(F32)
        bm = xbc_ref[:, _COL_B + g * SSD_D_STATE:_COL_B + (g + 1) * SSD_D_STATE]
        cm = xbc_ref[:, _COL_C + g * SSD_D_STATE:_COL_C + (g + 1) * SSD_D_STATE]
        st = state[g]
        yoff = jnp.dot(cm, st.astype(BF16), preferred_element_type=F32) * e_x[:, gs]
        xd = (xg * w_x[:, gs]).astype(BF16)
        upd = lax.dot_general(bm, xd, _TN, preferred_element_type=F32)
        state[g] = st * cd_x[:, gs] + upd
        yacc[:, gs] = yoff + dexp_ref[:, gs] * xg
        cbm = lax.dot_general(cm, bm, _NT, preferred_element_type=F32).astype(BF16)
        for jp in range(0, SSD_HEADS_PER_GROUP, 2):
            pair = []
            for j in (jp, jp + 1):
                h = g * SSD_HEADS_PER_GROUP + j
                seg = acs[:, h:h + 1] - acs_dt_t[h:h + 1, :]
                m = jnp.where(causal, jnp.exp2(seg), 0.0).astype(BF16) * cbm
                pair.append(jnp.dot(m, xb[:, j * SSD_HEAD_DIM:(j + 1) * SSD_HEAD_DIM],
                                    preferred_element_type=F32))
            c0 = g * SSD_GROUP_WIDTH + jp * SSD_HEAD_DIM
            yacc[:, c0:c0 + LANES] += jnp.concatenate(pair, axis=1)

    for g in range(SSD_N_GROUPS):
        gs = slice(g * SSD_GROUP_WIDTH, (g + 1) * SSD_GROUP_WIDTH)
        gv = yacc[:, gs] * _silu(z_ref[:, gs].astype(F32))
        ms = jnp.mean(gv * gv, axis=-1, keepdims=True)
        yn_ref[:, gs] = (gv * lax.rsqrt(ms + NORM_EPS) * nw_ref[:, gs]).astype(BF16)


def _ssd(z, xbc, dtp, dtb, alog, dexp, nw, e3, batch):
    t = z.shape[0]
    nc = t // batch // CHUNK
    row = lambda w: pl.BlockSpec((CHUNK, w), lambda b, c: (b * nc + c, 0))
    return pl.pallas_call(
        _ssd_kernel,
        grid=(batch, nc),
        in_specs=[row(SSD_D_INNER), row(SSD_CONV_DIM), row(DT_PAD),
                  _resident((1, DT_PAD)), _resident((1, DT_PAD)), _resident((1, SSD_D_INNER)),
                  _resident((1, SSD_D_INNER)), _resident((LANES, SSD_D_INNER))],
        out_specs=row(SSD_D_INNER),
        out_shape=jax.ShapeDtypeStruct((t, SSD_D_INNER), BF16),
        scratch_shapes=[pltpu.VMEM((SSD_N_GROUPS, SSD_D_STATE, SSD_GROUP_WIDTH), F32),
                        pltpu.VMEM((CHUNK, SSD_D_INNER), F32)],
        compiler_params=pltpu.CompilerParams(dimension_semantics=("arbitrary", "arbitrary"),
                                             vmem_limit_bytes=VMEM_LIMIT),
        name="ssd",
    )(z, xbc, dtp, dtb, alog, dexp, nw, e3)


def _proj_ssd_kernel(x_ref, nw_ref, wzx_ref, wg_ref, wdt_ref, wqkv_ref, cw_ref, cb_ref,
                     dtb_ref, alog_ref, dexp_ref, snw_ref, e3_ref,
                     gates_ref, qkvt_ref, yn_ref,
                     carry, tbuf, z_s, xbc_s, dt_s, state, yacc, *, tiles_per_seq, n_tiles):
    L = CHUNK
    j = pl.program_id(0)
    tm = x_ref.shape[0]
    slot = j % 2
    pslot = 1 - slot
    scan_tile = jnp.maximum(j - 1, 0)

    @pl.when(j == 0)
    def _():
        z_s[...] = jnp.zeros(z_s.shape, BF16)
        xbc_s[...] = jnp.zeros(xbc_s.shape, BF16)
        dt_s[...] = jnp.zeros(dt_s.shape, F32)
        state[...] = jnp.zeros(state.shape, F32)

    @pl.when(jnp.minimum(j, n_tiles - 1) % tiles_per_seq == 0)
    def _():
        carry[...] = jnp.zeros(carry.shape, F32)

    def proj_stages():
        x = x_ref[...]
        ms = jnp.mean(x * x, axis=-1, keepdims=True)
        u = (x * lax.rsqrt(ms + NORM_EPS) * nw_ref[...]).astype(BF16)
        yield

        def plain(store, w_ref, c, step):
            store(c, step, jnp.dot(u, w_ref[:, c:c + step], preferred_element_type=F32))

        def store_z(c, step, r):
            z_s[slot, :, c:c + step] = r.astype(BF16)

        def store_gates(c, step, r):
            gates_ref[:, c:c + step] = r.astype(BF16)

        def store_dt(c, step, r):
            dt_s[slot, :, c:c + step] = r

        def transposed(r0):
            tbuf[...] = jnp.dot(u, wqkv_ref[:, r0:r0 + IN_PROJ_TCHUNK], preferred_element_type=F32)
            qkvt_ref[r0:r0 + IN_PROJ_TCHUNK, :] = tbuf[...].T.astype(BF16)

        fillers = [functools.partial(plain, store, w_ref, c, min(IN_PROJ_NCHUNK, width))
                   for store, w_ref, width in ((store_z, wzx_ref, SSD_D_INNER), (store_gates, wg_ref, GATES_WIDTH),
                                               (store_dt, wdt_ref, DT_PAD))
                   for c in range(0, width, min(IN_PROJ_NCHUNK, width))]
        fillers += [functools.partial(transposed, r0) for r0 in range(0, QKV_WIDTH, IN_PROJ_TCHUNK)]
        n_conv = SSD_CONV_DIM // IN_PROJ_CONV_CHUNK
        per_conv = -(-len(fillers) // n_conv)
        for i in range(n_conv):
            c = i * IN_PROJ_CONV_CHUNK
            cs = slice(c, c + IN_PROJ_CONV_CHUNK)
            r = jnp.dot(u, wzx_ref[:, SSD_D_INNER + c:SSD_D_INNER + c + IN_PROJ_CONV_CHUNK],
                        preferred_element_type=F32)
            yield
            for f in fillers[i * per_conv:(i + 1) * per_conv]:
                f()
                yield
            acc = _causal_conv_rows(r, carry[:, cs], cw_ref[:, cs], cb_ref[:, cs])
            carry[:, cs] = r[tm - CONV_HALO:tm, :]
            xbc_s[slot, :, cs] = _silu(acc).astype(BF16)
            yield

    def scan_stages(c):
        rs = slice(c * L, (c + 1) * L)
        lane = lax.broadcasted_iota(jnp.int32, (L, LANES), 1)
        head_lane = lane < SSD_N_HEADS
        dtr = dt_s[pslot, rs, :] + dtb_ref[...]
        dt = jnp.maximum(dtr, 0.0) + jnp.log(1.0 + jnp.exp(-jnp.abs(dtr)))
        dt = jnp.where(head_lane, dt, 0.0)
        adt = dt * (-jnp.exp(alog_ref[...]))
        row_i = lax.broadcasted_iota(jnp.int32, (L, L), 0)
        col_i = lax.broadcasted_iota(jnp.int32, (L, L), 1)
        causal = col_i <= row_i
        tril = jnp.where(causal, 1.0, 0.0).astype(BF16)
        acs = sum(jnp.dot(tril, p.astype(BF16), preferred_element_type=F32) for p in _split3(adt)) * _LOG2E
        acs_dt_t = (acs - jnp.log2(dt)).T
        yield

        def pack3(v):
            hi, mid, lo = _split3(v)
            return jnp.where(head_lane, hi, jnp.where(lane < 2 * SSD_N_HEADS, pltpu.roll(mid, SSD_N_HEADS, 1),
                                                      pltpu.roll(lo, 2 * SSD_N_HEADS, 1))).astype(BF16)

        e_h = jnp.exp2(acs)
        w_h = dt * jnp.exp2(acs[L - 1:L, :] - acs)
        exp_in = jnp.concatenate([pack3(e_h), pack3(w_h)], axis=0)
        expd = jnp.dot(exp_in, e3_ref[...], preferred_element_type=F32)
        e_x = expd[0:L, :]
        w_x = expd[L:2 * L, :]
        cd_x = e_x[L - 1:L, :]
        yield
        reset = (scan_tile % tiles_per_seq) == 0
        for g in range(SSD_N_GROUPS):
            gs = slice(g * SSD_GROUP_WIDTH, (g + 1) * SSD_GROUP_WIDTH)
            xb = xbc_s[pslot, rs, gs]
            xg = xb.astype(F32)
            bm = xbc_s[pslot, rs, _COL_B + g * SSD_D_STATE:_COL_B + (g + 1) * SSD_D_STATE]
            cm = xbc_s[pslot, rs, _COL_C + g * SSD_D_STATE:_COL_C + (g + 1) * SSD_D_STATE]
            st = state[g]
            if c == 0:
                st = jnp.where(reset, 0.0, st)
            yoff = jnp.dot(cm, st.astype(BF16), preferred_element_type=F32) * e_x[:, gs]
            xd = (xg * w_x[:, gs]).astype(BF16)
            upd = lax.dot_general(bm, xd, _TN, preferred_element_type=F32)
            state[g] = st * cd_x[:, gs] + upd
            yacc[:, gs] = yoff + dexp_ref[:, gs] * xg
            cbm = lax.dot_general(cm, bm, _NT, preferred_element_type=F32).astype(BF16)
            yield
            for jp in range(0, SSD_HEADS_PER_GROUP, 2):
                pair = []
                for jh in (jp, jp + 1):
                    h = g * SSD_HEADS_PER_GROUP + jh
                    seg = acs[:, h:h + 1] - acs_dt_t[h:h + 1, :]
                    m = jnp.where(causal, jnp.exp2(seg), 0.0).astype(BF16) * cbm
                    pair.append(jnp.dot(m, xb[:, jh * SSD_HEAD_DIM:(jh + 1) * SSD_HEAD_DIM],
                                        preferred_element_type=F32))
                c0 = g * SSD_GROUP_WIDTH + jp * SSD_HEAD_DIM
                yacc[:, c0:c0 + LANES] += jnp.concatenate(pair, axis=1)
                yield
        for g in range(SSD_N_GROUPS):
            gs = slice(g * SSD_GROUP_WIDTH, (g + 1) * SSD_GROUP_WIDTH)
            gv = yacc[:, gs] * _silu(z_s[pslot, rs, gs].astype(F32))
            ms = jnp.mean(gv * gv, axis=-1, keepdims=True)
            yn_ref[rs, gs] = (gv * lax.rsqrt(ms + NORM_EPS) * snw_ref[:, gs]).astype(BF16)
            yield

    proj = proj_stages()
    scan = (st for c in range(tm // L) for st in scan_stages(c))
    n_proj = 1 + (SSD_CONV_DIM // IN_PROJ_CONV_CHUNK) * 2 + len(range(0, SSD_D_INNER, IN_PROJ_NCHUNK)) \
        + len(range(0, GATES_WIDTH, IN_PROJ_NCHUNK)) + 1 + len(range(0, QKV_WIDTH, IN_PROJ_TCHUNK))
    n_scan = (tm // L) * (2 + SSD_N_GROUPS * (1 + SSD_HEADS_PER_GROUP // 2) + SSD_N_GROUPS)
    done = 0
    for i in range(n_proj):
        next(proj)
        want = (i + 1) * n_scan // n_proj
        while done < want:
            next(scan)
            done += 1
    assert next(proj, "end") == "end" and next(scan, "end") == "end"


def _proj_ssd(x2, nw, w_zx, w_g, w_dt, w_qkv, cw, cb, dtb, alog, dexp, snw, e3, tm, batch):
    t = x2.shape[0]
    n_tiles = t // tm
    cur = lambda i: jnp.minimum(i, n_tiles - 1)
    prev = lambda i: jnp.maximum(i - 1, 0)
    return pl.pallas_call(
        functools.partial(_proj_ssd_kernel, tiles_per_seq=t // batch // tm, n_tiles=n_tiles),
        grid=(n_tiles + 1,),
        in_specs=[pl.BlockSpec((tm, D_MODEL), lambda i: (cur(i), 0)), _resident((1, D_MODEL)),
                  _resident((D_MODEL, SSD_D_INNER + SSD_CONV_DIM)), _resident((D_MODEL, GATES_WIDTH)),
                  _resident((D_MODEL, DT_PAD)), _resident((D_MODEL, QKV_WIDTH)),
                  _resident((SSD_CONV_WIDTH, SSD_CONV_DIM)), _resident((1, SSD_CONV_DIM)),
                  _resident((1, DT_PAD)), _resident((1, DT_PAD)), _resident((1, SSD_D_INNER)),
                  _resident((1, SSD_D_INNER)), _resident((LANES, SSD_D_INNER))],
        out_specs=[pl.BlockSpec((tm, GATES_WIDTH), lambda i: (cur(i), 0)),
                   pl.BlockSpec((QKV_WIDTH, tm), lambda i: (0, cur(i))),
                   pl.BlockSpec((tm, SSD_D_INNER), lambda i: (prev(i), 0))],
        out_shape=[jax.ShapeDtypeStruct((t, GATES_WIDTH), BF16), jax.ShapeDtypeStruct((QKV_WIDTH, t), BF16),
                   jax.ShapeDtypeStruct((t, SSD_D_INNER), BF16)],
        scratch_shapes=[pltpu.VMEM((CONV_HALO, SSD_CONV_DIM), F32), pltpu.VMEM((tm, IN_PROJ_TCHUNK), F32),
                        pltpu.VMEM((2, tm, SSD_D_INNER), BF16), pltpu.VMEM((2, tm, SSD_CONV_DIM), BF16),
                        pltpu.VMEM((2, tm, DT_PAD), F32),
                        pltpu.VMEM((SSD_N_GROUPS, SSD_D_STATE, SSD_GROUP_WIDTH), F32),
                        pltpu.VMEM((CHUNK, SSD_D_INNER), F32)],
        compiler_params=pltpu.CompilerParams(dimension_semantics=("arbitrary",), vmem_limit_bytes=VMEM_V7X),
        name="proj_ssd",
    )(x2, nw, w_zx, w_g, w_dt, w_qkv, cw, cb, dtb, alog, dexp, snw, e3)


ROPE_HALF = ATTN_HEAD_DIM // 2


def _attn_kernel(sink_ref, qkvt_ref, pos_ref, invf_ref, aot_ref, qt, kk, vvt):
    L = CHUNK
    blk = pl.program_id(1)

    @pl.when(blk == 0)
    def _():
        kk[:, L:2 * L, :] = jnp.zeros((ATTN_N_KV_HEADS, L, ATTN_HEAD_DIM), BF16)
        vvt[:, L:2 * L] = jnp.zeros((KV_WIDTH, L), BF16)

    @pl.when(blk > 0)
    def _():
        kk[:, L:2 * L, :] = kk[:, 0:L, :]
        vvt[:, L:2 * L] = vvt[:, 0:L]

    ang = invf_ref[...] * pos_ref[...].astype(F32)
    cosv = jnp.cos(ang)
    sinv = jnp.sin(ang)

    def rope(r0):
        t1 = qkvt_ref[r0:r0 + ROPE_HALF, :].astype(F32)
        t2 = qkvt_ref[r0 + ROPE_HALF:r0 + ATTN_HEAD_DIM, :].astype(F32)
        return jnp.concatenate([t1 * cosv - t2 * sinv, t2 * cosv + t1 * sinv], axis=0)

    scale = ATTN_HEAD_DIM ** -0.5 * _LOG2E
    for h in range(ATTN_N_HEADS):
        r0 = h * ATTN_HEAD_DIM
        qt[r0:r0 + ATTN_HEAD_DIM, :] = (rope(r0) * scale).astype(BF16)
    for g in range(ATTN_N_KV_HEADS):
        kk[g, 0:L, :] = rope(ATTN_WIDTH + g * ATTN_HEAD_DIM).T.astype(BF16)
    vvt[:, 0:L] = qkvt_ref[ATTN_WIDTH + KV_WIDTH:QKV_WIDTH, :]

    cols = ATTN_REP * L
    key_i = lax.broadcasted_iota(jnp.int32, (L, cols), 0)
    qry_i = lax.broadcasted_iota(jnp.int32, (L, cols), 1) % L
    in_cur = key_i <= qry_i
    old_bias = jnp.where(blk > 0, 0.0, -jnp.inf)
    groups = range(ATTN_N_KV_HEADS)
    heads = [range(g * ATTN_REP, (g + 1) * ATTN_REP) for g in groups]
    s2 = [jnp.dot(kk[g], jnp.concatenate([qt[h * ATTN_HEAD_DIM:(h + 1) * ATTN_HEAD_DIM, :] for h in heads[g]],
                                         axis=1), preferred_element_type=F32) for g in groups]
    s = [jnp.where(in_cur, s2[g][0:L], s2[g][L:2 * L] + old_bias) for g in groups]
    sink = [jnp.concatenate([jnp.full((1, L), sink_ref[h] * _LOG2E, F32) for h in heads[g]], axis=1) for g in groups]
    m = [jnp.maximum(jnp.max(s[g], axis=0, keepdims=True), sink[g]) for g in groups]
    p = [jnp.exp2(s[g] - m[g]) for g in groups]
    denom = [jnp.sum(p[g], axis=0, keepdims=True) + jnp.exp2(sink[g] - m[g]) for g in groups]
    pt = [jnp.concatenate([jnp.where(in_cur, p[g], 0.0), jnp.where(in_cur, 0.0, p[g])], axis=0).astype(BF16)
          for g in groups]
    o = [jnp.dot(vvt[g * ATTN_HEAD_DIM:(g + 1) * ATTN_HEAD_DIM, :], pt[g], preferred_element_type=F32)
         * (1.0 / denom[g]) for g in groups]
    for g in groups:
        for r, h in enumerate(heads[g]):
            aot_ref[h * ATTN_HEAD_DIM:(h + 1) * ATTN_HEAD_DIM, :] = o[g][:, r * L:(r + 1) * L].astype(BF16)


def _attn(sinks, qkvt, pos_row, invf, batch):
    t = qkvt.shape[1]
    nb = t // batch // CHUNK
    col = lambda h: pl.BlockSpec((h, CHUNK), lambda b, c: (0, b * nb + c))
    return pl.pallas_call(
        _attn_kernel,
        grid=(batch, nb),
        in_specs=[pl.BlockSpec(memory_space=pltpu.SMEM), col(QKV_WIDTH), col(1), _resident((ROPE_HALF, CHUNK))],
        out_specs=col(ATTN_WIDTH),
        out_shape=jax.ShapeDtypeStruct((ATTN_WIDTH, t), BF16),
        scratch_shapes=[pltpu.VMEM((ATTN_WIDTH, CHUNK), BF16),
                        pltpu.VMEM((ATTN_N_KV_HEADS, 2 * CHUNK, ATTN_HEAD_DIM), BF16),
                        pltpu.VMEM((KV_WIDTH, 2 * CHUNK), BF16)],
        compiler_params=pltpu.CompilerParams(dimension_semantics=("arbitrary", "arbitrary"),
                                             vmem_limit_bytes=VMEM_LIMIT),
        name="attn",
    )(sinks, qkvt, pos_row, invf)


MIX_NCHUNK = 256


def _attn_mix_kernel(sink_ref, qkvt_ref, pos_ref, invf_ref, yn_ref, gates_ref, x_ref, wso_ref, wao_ref, wmix_ref,
                     nw_ref, o_ref, qt, kk, vvt, ao_s, *, tiles_per_seq, n_tiles):
    L = CHUNK
    j = pl.program_id(0)
    tm = x_ref.shape[0]
    first_tile = (jnp.minimum(j, n_tiles - 1) % tiles_per_seq) == 0

    @pl.when(j == 0)
    def _():
        ao_s[...] = jnp.zeros(ao_s.shape, BF16)
        kk[...] = jnp.zeros(kk.shape, BF16)
        vvt[...] = jnp.zeros(vvt.shape, BF16)

    slot = j % 2
    cols = ATTN_REP * L
    key_i = lax.broadcasted_iota(jnp.int32, (L, cols), 0)
    qry_i = lax.broadcasted_iota(jnp.int32, (L, cols), 1) % L
    in_cur = key_i <= qry_i
    scale = ATTN_HEAD_DIM ** -0.5 * _LOG2E
    groups = range(ATTN_N_KV_HEADS)
    heads = [range(g * ATTN_REP, (g + 1) * ATTN_REP) for g in groups]
    sink = [jnp.concatenate([jnp.full((1, L), sink_ref[h] * _LOG2E, F32) for h in heads[g]], axis=1) for g in groups]

    def attn_stages(b):
        ts = slice(b * L, (b + 1) * L)
        pc, po = b % 2, 1 - b % 2
        ang = invf_ref[...] * pos_ref[:, ts].astype(F32)
        cosv = jnp.cos(ang)
        sinv = jnp.sin(ang)

        def rope(r0):
            t1 = qkvt_ref[r0:r0 + ROPE_HALF, ts].astype(F32)
            t2 = qkvt_ref[r0 + ROPE_HALF:r0 + ATTN_HEAD_DIM, ts].astype(F32)
            return jnp.concatenate([t1 * cosv - t2 * sinv, t2 * cosv + t1 * sinv], axis=0)

        for h in range(ATTN_N_HEADS):
            r0 = h * ATTN_HEAD_DIM
            qt[r0:r0 + ATTN_HEAD_DIM, :] = (rope(r0) * scale).astype(BF16)
            if h % 8 == 7:
                yield
        for g in groups:
            kk[pc, g] = rope(ATTN_WIDTH + g * ATTN_HEAD_DIM).T.astype(BF16)
        vvt[pc] = qkvt_ref[ATTN_WIDTH + KV_WIDTH:QKV_WIDTH, ts]
        yield
        old_bias = jnp.where(first_tile, -jnp.inf, 0.0) if b == 0 else 0.0
        s2 = [jnp.dot(jnp.concatenate([kk[pc, g], kk[po, g]], axis=0),
                      jnp.concatenate([qt[h * ATTN_HEAD_DIM:(h + 1) * ATTN_HEAD_DIM, :] for h in heads[g]], axis=1),
                      preferred_element_type=F32) for g in groups]
        yield
        s = [jnp.where(in_cur, s2[g][0:L], s2[g][L:2 * L] + old_bias) for g in groups]
        m = [jnp.maximum(jnp.max(s[g], axis=0, keepdims=True), sink[g]) for g in groups]
        yield
        p = [jnp.exp2(s[g] - m[g]) for g in groups]
        denom = [jnp.sum(p[g], axis=0, keepdims=True) + jnp.exp2(sink[g] - m[g]) for g in groups]
        yield
        pt = [jnp.concatenate([jnp.where(in_cur, p[g], 0.0), jnp.where(in_cur, 0.0, p[g])], axis=0).astype(BF16)
              for g in groups]
        yield
        vs = [slice(g * ATTN_HEAD_DIM, (g + 1) * ATTN_HEAD_DIM) for g in groups]
        o = [jnp.dot(jnp.concatenate([vvt[pc, vs[g], :], vvt[po, vs[g], :]], axis=1), pt[g],
                     preferred_element_type=F32) * (1.0 / denom[g]) for g in groups]
        for g in groups:
            for r, h in enumerate(heads[g]):
                ao_s[slot, b, h * ATTN_HEAD_DIM:(h + 1) * ATTN_HEAD_DIM, :] = o[g][:, r * L:(r + 1) * L].astype(BF16)
        yield

    def mix_stages():
        nc = D_MODEL // MIX_NCHUNK
        csl = [slice(c * MIX_NCHUNK, (c + 1) * MIX_NCHUNK) for c in range(nc)]
        yn = yn_ref[...]
        ys = []
        for c in range(nc):
            ys.append(jnp.dot(yn, wso_ref[:, csl[c]], preferred_element_type=F32))
            yield
        ya = jnp.concatenate([lax.dot_general(ao_s[1 - slot, b], wao_ref[...], _TN, preferred_element_type=F32)
                              for b in range(tm // L)], axis=0)
        yield
        merged = []
        for c in range(nc):
            gs = _sigmoid(gates_ref[:, csl[c]].astype(F32))
            ga = _sigmoid(gates_ref[:, D_MODEL + c * MIX_NCHUNK:D_MODEL + (c + 1) * MIX_NCHUNK].astype(F32))
            merged.append((gs * ys[c] + ga * ya[:, csl[c]]).astype(BF16))
        merged = jnp.concatenate(merged, axis=1)
        yield
        mo = []
        for c in range(nc):
            mo.append(jnp.dot(merged, wmix_ref[:, csl[c]], preferred_element_type=F32))
            yield
        mo = jnp.concatenate(mo, axis=1)
        ms = jnp.mean(mo * mo, axis=-1, keepdims=True)
        o_ref[...] = x_ref[...] + mo * lax.rsqrt(ms + NORM_EPS) * nw_ref[...]
        yield

    att = (st for b in range(tm // L) for st in attn_stages(b))
    mix = mix_stages()
    n_att, n_mix = (tm // L) * 8, 2 * (D_MODEL // MIX_NCHUNK) + 3
    done_att = 0
    for i in range(n_mix):
        next(mix)
        want = (i + 1) * n_att // n_mix
        while done_att < want:
            next(att)
            done_att += 1
    assert next(att, "end") == "end" and next(mix, "end") == "end"


def _attn_mix(sinks, qkvt, pos_row, invf, yn, gates, x2, wso, wao, wmix, nw, tm, batch):
    t = x2.shape[0]
    n_tiles = t // tm
    assert tm == ATTN_REP * CHUNK
    cur = lambda h: pl.BlockSpec((h, tm), lambda j: (0, jnp.minimum(j, n_tiles - 1)))
    prev = lambda w: pl.BlockSpec((tm, w), lambda j: (jnp.maximum(j - 1, 0), 0))
    return pl.pallas_call(
        functools.partial(_attn_mix_kernel, tiles_per_seq=t // batch // tm, n_tiles=n_tiles),
        grid=(n_tiles + 1,),
        in_specs=[pl.BlockSpec(memory_space=pltpu.SMEM), cur(QKV_WIDTH), cur(1), _resident((ROPE_HALF, CHUNK)),
                  prev(SSD_D_INNER), prev(GATES_WIDTH), prev(D_MODEL),
                  _resident((SSD_D_INNER, D_MODEL)), _resident((ATTN_WIDTH, D_MODEL)),
                  _resident((D_MODEL, D_MODEL)), _resident((1, D_MODEL))],
        out_specs=prev(D_MODEL),
        out_shape=jax.ShapeDtypeStruct((t, D_MODEL), F32),
        scratch_shapes=[pltpu.VMEM((ATTN_WIDTH, CHUNK), BF16),
                        pltpu.VMEM((2, ATTN_N_KV_HEADS, CHUNK, ATTN_HEAD_DIM), BF16),
                        pltpu.VMEM((2, KV_WIDTH, CHUNK), BF16),
                        pltpu.VMEM((2, tm // CHUNK, ATTN_WIDTH, CHUNK), BF16)],
        compiler_params=pltpu.CompilerParams(dimension_semantics=("arbitrary",), vmem_limit_bytes=VMEM_LIMIT),
        name="attn_mix",
    )(sinks, qkvt, pos_row, invf, yn, gates, x2, wso, wao, wmix, nw)


def _mix_out_kernel(yn_ref, aot_ref, gates_ref, x_ref, wso_ref, wao_ref, wmix_ref, nw_ref, o_ref):
    ys = jnp.dot(yn_ref[...], wso_ref[...], preferred_element_type=F32)
    ya = lax.dot_general(aot_ref[...], wao_ref[...], _TN, preferred_element_type=F32)
    gs = _sigmoid(gates_ref[:, 0:D_MODEL].astype(F32))
    ga = _sigmoid(gates_ref[:, D_MODEL:GATES_WIDTH].astype(F32))
    merged = (gs * ys + ga * ya).astype(BF16)
    mo = jnp.dot(merged, wmix_ref[...], preferred_element_type=F32)
    ms = jnp.mean(mo * mo, axis=-1, keepdims=True)
    o_ref[...] = x_ref[...] + mo * lax.rsqrt(ms + NORM_EPS) * nw_ref[...]


def _mix_out(yn, ao, gates, x2, wso, wao, wmix, nw, tm):
    t = x2.shape[0]
    row = lambda w: pl.BlockSpec((tm, w), lambda i: (i, 0))
    return pl.pallas_call(
        _mix_out_kernel,
        grid=(t // tm,),
        in_specs=[row(SSD_D_INNER), pl.BlockSpec((ATTN_WIDTH, tm), lambda i: (0, i)), row(GATES_WIDTH), row(D_MODEL),
                  _resident((SSD_D_INNER, D_MODEL)), _resident((ATTN_WIDTH, D_MODEL)),
                  _resident((D_MODEL, D_MODEL)), _resident((1, D_MODEL))],
        out_specs=row(D_MODEL),
        out_shape=jax.ShapeDtypeStruct((t, D_MODEL), F32),
        compiler_params=pltpu.CompilerParams(dimension_semantics=("arbitrary",), vmem_limit_bytes=VMEM_LIMIT),
        name="mix_out",
    )(yn, ao, gates, x2, wso, wao, wmix, nw)


FFN_NCHUNK = 256
FFN_SUBTILE = 256
_GELU_A = float(-2.0 * np.sqrt(2.0 / np.pi) * np.log2(np.e))
_GELU_B = float(_GELU_A * 0.044715)


def _gelu_tanh(v):
    return v / (1.0 + jnp.exp2(v * (_GELU_A + _GELU_B * (v * v))))


def _ffn_kernel(x_ref, npre_ref, wup_ref, cw_ref, cb_ref, wdn_ref, npost_ref, o_ref, carry, act,
                *, tiles_per_seq):
    tm = x_ref.shape[0]
    ts = min(FFN_SUBTILE, tm)

    @pl.when(pl.program_id(0) % tiles_per_seq == 0)
    def _():
        carry[...] = jnp.zeros(carry.shape, F32)

    def up(h, c0):
        return (jnp.dot(h, wup_ref[:, c0:c0 + FFN_NCHUNK], preferred_element_type=F32),
                jnp.dot(h, wup_ref[:, FFN_D_FF + c0:FFN_D_FF + c0 + FFN_NCHUNK], preferred_element_type=F32))

    def conv_chunk(r, c0):
        cs = slice(c0, c0 + FFN_NCHUNK)
        out = _causal_conv_rows(r, carry[:, cs], cw_ref[:, cs], cb_ref[:, cs])
        carry[:, cs] = r[ts - CONV_HALO:ts, :]
        return out

    chunks = list(range(0, FFN_D_FF, FFN_NCHUNK))
    for r0 in range(0, tm, ts):
        x = x_ref[r0:r0 + ts, :]
        ms = jnp.mean(x * x, axis=-1, keepdims=True)
        h = (x * lax.rsqrt(ms + NORM_EPS) * npre_ref[...]).astype(BF16)
        nxt = up(h, chunks[0])
        for i, c0 in enumerate(chunks):
            cur = nxt
            if i + 1 < len(chunks):
                nxt = up(h, chunks[i + 1])
            gate = conv_chunk(cur[0], c0).astype(BF16)
            val = conv_chunk(cur[1], FFN_D_FF + c0).astype(BF16)
            act[r0:r0 + ts, c0:c0 + FFN_NCHUNK] = _gelu_tanh(gate) * val
        ff = jnp.dot(act[r0:r0 + ts, :], wdn_ref[...], preferred_element_type=F32)
        ms2 = jnp.mean(ff * ff, axis=-1, keepdims=True)
        o_ref[r0:r0 + ts, :] = x + ff * lax.rsqrt(ms2 + NORM_EPS) * npost_ref[...]


def _ffn(x1, npre, wup, cw, cb, wdn, npost, tm, batch):
    t = x1.shape[0]
    row = lambda w: pl.BlockSpec((tm, w), lambda i: (i, 0))
    return pl.pallas_call(
        functools.partial(_ffn_kernel, tiles_per_seq=t // batch // tm),
        grid=(t // tm,),
        in_specs=[row(D_MODEL), _resident((1, D_MODEL)), _resident((D_MODEL, 2 * FFN_D_FF)),
                  _resident((FFN_CONV_WIDTH, 2 * FFN_D_FF)), _resident((1, 2 * FFN_D_FF)),
                  _resident((FFN_D_FF, D_MODEL)), _resident((1, D_MODEL))],
        out_specs=row(D_MODEL),
        out_shape=jax.ShapeDtypeStruct((t, D_MODEL), F32),
        scratch_shapes=[pltpu.VMEM((CONV_HALO, 2 * FFN_D_FF), F32),
                        pltpu.VMEM((tm, FFN_D_FF), BF16)],
        compiler_params=pltpu.CompilerParams(dimension_semantics=("arbitrary",), vmem_limit_bytes=VMEM_LIMIT),
        name="ffn",
    )(x1, npre, wup, cw, cb, wdn, npost)


def _expansion_matrix():
    e = np.zeros((LANES, SSD_D_INNER), np.float32)
    ch = np.arange(SSD_D_INNER)
    for part in range(3):
        e[part * SSD_N_HEADS + ch // SSD_HEAD_DIM, ch] = 1.0
    return jnp.asarray(e, dtype=BF16)


def _rope_inv_freq():
    inv = ROPE_THETA ** (-jnp.arange(ROPE_HALF, dtype=F32) * 2.0 / ATTN_HEAD_DIM)
    return jnp.broadcast_to(inv[:, None], (ROPE_HALF, CHUNK))


def _layer(x2, pos_row, batch, norm_mix_pre_w, w_in, ssd_conv_w, ssd_conv_b, ssd_dt_bias, ssd_a_log, ssd_d,
           ssd_norm_w, ssd_w_out, attn_sinks, attn_w_out, w_mix_out, norm_mix_post_w, norm_ffn_pre_w,
           ffn_w_up, ffn_conv_w, ffn_conv_b, ffn_w_down, norm_ffn_post_w, tm):
    o = np.cumsum((0, SSD_D_INNER, SSD_CONV_DIM, SSD_N_HEADS, ATTN_WIDTH, KV_WIDTH, KV_WIDTH, D_MODEL, D_MODEL))
    w_zx = w_in[:, o[0]:o[2]].astype(BF16)
    w_g = w_in[:, o[6]:o[8]].astype(BF16)
    w_dt = jnp.pad(w_in[:, o[2]:o[3]], ((0, 0), (0, DT_PAD - SSD_N_HEADS))).astype(BF16)
    w_qkv = w_in[:, o[3]:o[6]].astype(BF16)
    row = lambda v: v.reshape(1, -1).astype(F32)
    pad_heads = lambda v: jnp.pad(v.astype(F32), (0, DT_PAD - SSD_N_HEADS)).reshape(1, DT_PAD)

    gates, qkvt, yn = _proj_ssd(x2, row(norm_mix_pre_w), w_zx, w_g, w_dt, w_qkv, ssd_conv_w.astype(F32),
                                row(ssd_conv_b), pad_heads(ssd_dt_bias), pad_heads(ssd_a_log),
                                row(jnp.repeat(ssd_d, SSD_HEAD_DIM)), row(ssd_norm_w), _expansion_matrix(), tm, batch)
    x1 = _attn_mix(attn_sinks.astype(F32), qkvt, pos_row, _rope_inv_freq(), yn, gates, x2, ssd_w_out.astype(BF16),
                   attn_w_out.astype(BF16), w_mix_out.astype(BF16), row(norm_mix_post_w), tm, batch)
    return _ffn(x1, row(norm_ffn_pre_w), ffn_w_up.astype(BF16), ffn_conv_w.astype(F32), row(ffn_conv_b),
                ffn_w_down.astype(BF16), row(norm_ffn_post_w), tm, batch)


def kernel(x, positions, norm_mix_pre_w, w_in, ssd_conv_w, ssd_conv_b, ssd_dt_bias, ssd_a_log, ssd_d, ssd_norm_w,
           ssd_w_out, attn_sinks, attn_w_out, w_mix_out, norm_mix_post_w, norm_ffn_pre_w, ffn_w_up, ffn_conv_w,
           ffn_conv_b, ffn_w_down, norm_ffn_post_w):
    batch, seq, d = x.shape
    assert d == D_MODEL and seq % CHUNK == 0
    tm = 512 if seq % 512 == 0 else CHUNK
    x2 = x.reshape(batch * seq, d)
    pos_row = positions.reshape(1, batch * seq)
    for i in range(w_in.shape[0]):
        x2 = _layer(x2, pos_row, batch, norm_mix_pre_w[i], w_in[i], ssd_conv_w[i], ssd_conv_b[i], ssd_dt_bias[i],
                    ssd_a_log[i], ssd_d[i], ssd_norm_w[i], ssd_w_out[i], attn_sinks[i], attn_w_out[i],
                    w_mix_out[i], norm_mix_post_w[i], norm_ffn_pre_w[i], ffn_w_up[i], ffn_conv_w[i],
                    ffn_conv_b[i], ffn_w_down[i], norm_ffn_post_w[i], tm)
    return x2.reshape(batch, seq, d)
```

```python
import functools

import numpy as np
import jax
import jax.numpy as jnp
from jax import lax
from jax.experimental import pallas as pl
from jax.experimental.pallas import tpu as pltpu

F32 = jnp.float32
BF16 = jnp.bfloat16

D_MODEL = 1024
SSD_D_INNER = 2048
SSD_HEAD_DIM = 64
SSD_N_HEADS = 32
SSD_N_GROUPS = 4
SSD_HEADS_PER_GROUP = 8
SSD_D_STATE = 128
SSD_CONV_WIDTH = 4
SSD_CONV_DIM = 3072
SSD_GROUP_WIDTH = SSD_D_INNER // SSD_N_GROUPS
CHUNK = 128

ATTN_HEAD_DIM = 64
ATTN_N_HEADS = 16
ATTN_N_KV_HEADS = 4
ATTN_REP = 4
ATTN_WIDTH = 1024
KV_WIDTH = 256
QKV_WIDTH = ATTN_WIDTH + 2 * KV_WIDTH
ROPE_THETA = 10000.0
ROPE_HALF = ATTN_HEAD_DIM // 2

FFN_D_FF = 2816
FFN_CONV_WIDTH = 3
NORM_EPS = 1e-6

LANES = 128
SUBLANES = 8
DT_PAD = LANES
GATES_WIDTH = 2 * D_MODEL
TILE = ATTN_REP * CHUNK

VMEM_V7X = 64 * 1024 * 1024
VMEM_LIMIT = 56 * 1024 * 1024

_NT = (((1,), (1,)), ((), ()))
_TN = (((0,), (0,)), ((), ()))
_LOG2E = float(np.log2(np.e))
CONV_HALO = SUBLANES


def _resident(shape):
    nd = len(shape)
    return pl.BlockSpec(shape, lambda *_: (0,) * nd, pipeline_mode=pl.Buffered(1))


def _sigmoid(v):
    return 1.0 / (1.0 + jnp.exp(-v))


def _silu(v):
    return v / (1.0 + jnp.exp2(v * -_LOG2E))


def _split3(v):
    hi = v.astype(BF16).astype(F32)
    r = v - hi
    mid = r.astype(BF16).astype(F32)
    lo = (r - mid).astype(BF16).astype(F32)
    return hi, mid, lo


def _causal_conv_rows(r, halo, w, b):
    taps, (rows, width) = w.shape[0], r.shape
    nblk = rows // SUBLANES
    full = jnp.concatenate([halo, r], axis=0).reshape(nblk + 1, SUBLANES, width)
    sub = lax.broadcasted_iota(jnp.int32, (nblk, SUBLANES, width), 1)
    out = b + w[taps - 1:taps, :] * r
    for s in range(1, taps):
        merged = jnp.where(sub >= SUBLANES - s, full[0:nblk], full[1:nblk + 1])
        shifted = pltpu.roll(merged, s, 1).reshape(rows, width)
        out = out + w[taps - 1 - s:taps - s, :] * shifted
    return out


def _deal(major, n_major, minor, n_minor):
    done = 0
    for i in range(n_major):
        next(major)
        want = (i + 1) * n_minor // n_major
        while done < want:
            next(minor)
            done += 1
    assert next(major, "end") == "end" and next(minor, "end") == "end"


IN_PROJ_NCHUNK = 256
IN_PROJ_CONV_CHUNK = 256
IN_PROJ_TCHUNK = 256
_COL_B = SSD_D_INNER
_COL_C = SSD_D_INNER + SSD_N_GROUPS * SSD_D_STATE


def _proj_ssd_kernel(x_ref, nw_ref, wzx_ref, wg_ref, wdt_ref, wqkv_ref, cw_ref, cb_ref,
                     dtb_ref, alog_ref, dexp_ref, snw_ref, e3_ref,
                     gates_ref, qkvt_ref, yn_ref,
                     carry, tbuf, z_s, xbc_s, dt_s, state, yacc, *, tiles_per_seq, n_tiles):
    L = CHUNK
    j = pl.program_id(0)
    tm = x_ref.shape[0]
    slot = j % 2
    pslot = 1 - slot
    scan_tile = jnp.maximum(j - 1, 0)

    @pl.when(j == 0)
    def _():
        z_s[...] = jnp.zeros(z_s.shape, BF16)
        xbc_s[...] = jnp.zeros(xbc_s.shape, BF16)
        dt_s[...] = jnp.zeros(dt_s.shape, F32)
        state[...] = jnp.zeros(state.shape, F32)

    @pl.when(jnp.minimum(j, n_tiles - 1) % tiles_per_seq == 0)
    def _():
        carry[...] = jnp.zeros(carry.shape, F32)

    def proj_stages():
        x = x_ref[...]
        ms = jnp.mean(x * x, axis=-1, keepdims=True)
        u = (x * lax.rsqrt(ms + NORM_EPS) * nw_ref[...]).astype(BF16)
        yield

        def plain(store, w_ref, c, step):
            store(c, step, jnp.dot(u, w_ref[:, c:c + step], preferred_element_type=F32))

        def store_z(c, step, r):
            z_s[slot, :, c:c + step] = r.astype(BF16)

        def store_gates(c, step, r):
            gates_ref[:, c:c + step] = r.astype(BF16)

        def store_dt(c, step, r):
            dt_s[slot, :, c:c + step] = r

        def transposed(r0):
            tbuf[...] = jnp.dot(u, wqkv_ref[:, r0:r0 + IN_PROJ_TCHUNK], preferred_element_type=F32)
            qkvt_ref[r0:r0 + IN_PROJ_TCHUNK, :] = tbuf[...].T.astype(BF16)

        fillers = [functools.partial(plain, store, w_ref, c, min(IN_PROJ_NCHUNK, width))
                   for store, w_ref, width in ((store_z, wzx_ref, SSD_D_INNER), (store_gates, wg_ref, GATES_WIDTH),
                                               (store_dt, wdt_ref, DT_PAD))
                   for c in range(0, width, min(IN_PROJ_NCHUNK, width))]
        fillers += [functools.partial(transposed, r0) for r0 in range(0, QKV_WIDTH, IN_PROJ_TCHUNK)]
        n_conv = SSD_CONV_DIM // IN_PROJ_CONV_CHUNK
        per_conv = -(-len(fillers) // n_conv)
        for i in range(n_conv):
            c = i * IN_PROJ_CONV_CHUNK
            cs = slice(c, c + IN_PROJ_CONV_CHUNK)
            r = jnp.dot(u, wzx_ref[:, SSD_D_INNER + c:SSD_D_INNER + c + IN_PROJ_CONV_CHUNK],
                        preferred_element_type=F32)
            yield
            for f in fillers[i * per_conv:(i + 1) * per_conv]:
                f()
                yield
            acc = _causal_conv_rows(r, carry[:, cs], cw_ref[:, cs], cb_ref[:, cs])
            carry[:, cs] = r[tm - CONV_HALO:tm, :]
            xbc_s[slot, :, cs] = _silu(acc).astype(BF16)
            yield

    def scan_stages(c):
        rs = slice(c * L, (c + 1) * L)
        lane = lax.broadcasted_iota(jnp.int32, (L, LANES), 1)
        head_lane = lane < SSD_N_HEADS
        dtr = dt_s[pslot, rs, :] + dtb_ref[...]
        dt = jnp.maximum(dtr, 0.0) + jnp.log(1.0 + jnp.exp(-jnp.abs(dtr)))
        dt = jnp.where(head_lane, dt, 0.0)
        adt = dt * (-jnp.exp(alog_ref[...]))
        row_i = lax.broadcasted_iota(jnp.int32, (L, L), 0)
        col_i = lax.broadcasted_iota(jnp.int32, (L, L), 1)
        causal = col_i <= row_i
        tril = jnp.where(causal, 1.0, 0.0).astype(BF16)
        acs = sum(jnp.dot(tril, p.astype(BF16), preferred_element_type=F32) for p in _split3(adt)) * _LOG2E
        acs_dt_t = (acs - jnp.log2(dt)).T
        yield

        def pack3(v):
            hi, mid, lo = _split3(v)
            return jnp.where(head_lane, hi, jnp.where(lane < 2 * SSD_N_HEADS, pltpu.roll(mid, SSD_N_HEADS, 1),
                                                      pltpu.roll(lo, 2 * SSD_N_HEADS, 1))).astype(BF16)

        e_h = jnp.exp2(acs)
        w_h = dt * jnp.exp2(acs[L - 1:L, :] - acs)
        exp_in = jnp.concatenate([pack3(e_h), pack3(w_h)], axis=0)
        expd = jnp.dot(exp_in, e3_ref[...], preferred_element_type=F32)
        e_x = expd[0:L, :]
        w_x = expd[L:2 * L, :]
        cd_x = e_x[L - 1:L, :]
        yield
        reset = (scan_tile % tiles_per_seq) == 0
        for g in range(SSD_N_GROUPS):
            gs = slice(g * SSD_GROUP_WIDTH, (g + 1) * SSD_GROUP_WIDTH)
            xb = xbc_s[pslot, rs, gs]
            xg = xb.astype(F32)
            bm = xbc_s[pslot, rs, _COL_B + g * SSD_D_STATE:_COL_B + (g + 1) * SSD_D_STATE]
            cm = xbc_s[pslot, rs, _COL_C + g * SSD_D_STATE:_COL_C + (g + 1) * SSD_D_STATE]
            st = state[g]
            if c == 0:
                st = jnp.where(reset, 0.0, st)
            yoff = jnp.dot(cm, st.astype(BF16), preferred_element_type=F32) * e_x[:, gs]
            xd = (xg * w_x[:, gs]).astype(BF16)
            upd = lax.dot_general(bm, xd, _TN, preferred_element_type=F32)
            state[g] = st * cd_x[:, gs] + upd
            yacc[:, gs] = yoff + dexp_ref[:, gs] * xg
            cbm = lax.dot_general(cm, bm, _NT, preferred_element_type=F32).astype(BF16)
            yield
            for jp in range(0, SSD_HEADS_PER_GROUP, 2):
                pair = []
                for jh in (jp, jp + 1):
                    h = g * SSD_HEADS_PER_GROUP + jh
                    seg = acs[:, h:h + 1] - acs_dt_t[h:h + 1, :]
                    m = jnp.where(causal, jnp.exp2(seg), 0.0).astype(BF16) * cbm
                    pair.append(jnp.dot(m, xb[:, jh * SSD_HEAD_DIM:(jh + 1) * SSD_HEAD_DIM],
                                        preferred_element_type=F32))
                c0 = g * SSD_GROUP_WIDTH + jp * SSD_HEAD_DIM
                yacc[:, c0:c0 + LANES] += jnp.concatenate(pair, axis=1)
                yield
        for g in range(SSD_N_GROUPS):
            gs = slice(g * SSD_GROUP_WIDTH, (g + 1) * SSD_GROUP_WIDTH)
            gv = yacc[:, gs] * _silu(z_s[pslot, rs, gs].astype(F32))
            ms = jnp.mean(gv * gv, axis=-1, keepdims=True)
            yn_ref[rs, gs] = (gv * lax.rsqrt(ms + NORM_EPS) * snw_ref[:, gs]).astype(BF16)
            yield

    n_proj = 1 + (SSD_CONV_DIM // IN_PROJ_CONV_CHUNK) * 2 + len(range(0, SSD_D_INNER, IN_PROJ_NCHUNK)) \
        + len(range(0, GATES_WIDTH, IN_PROJ_NCHUNK)) + 1 + len(range(0, QKV_WIDTH, IN_PROJ_TCHUNK))
    n_scan = (tm // L) * (2 + SSD_N_GROUPS * (1 + SSD_HEADS_PER_GROUP // 2) + SSD_N_GROUPS)
    _deal(proj_stages(), n_proj, (st for c in range(tm // L) for st in scan_stages(c)), n_scan)


def _proj_ssd(x2, nw, w_zx, w_g, w_dt, w_qkv, cw, cb, dtb, alog, dexp, snw, e3, tm, batch):
    t = x2.shape[0]
    n_tiles = t // tm
    cur = lambda i: jnp.minimum(i, n_tiles - 1)
    prev = lambda i: jnp.maximum(i - 1, 0)
    return pl.pallas_call(
        functools.partial(_proj_ssd_kernel, tiles_per_seq=t // batch // tm, n_tiles=n_tiles),
        grid=(n_tiles + 1,),
        in_specs=[pl.BlockSpec((tm, D_MODEL), lambda i: (cur(i), 0)), _resident((1, D_MODEL)),
                  _resident((D_MODEL, SSD_D_INNER + SSD_CONV_DIM)), _resident((D_MODEL, GATES_WIDTH)),
                  _resident((D_MODEL, DT_PAD)), _resident((D_MODEL, QKV_WIDTH)),
                  _resident((SSD_CONV_WIDTH, SSD_CONV_DIM)), _resident((1, SSD_CONV_DIM)),
                  _resident((1, DT_PAD)), _resident((1, DT_PAD)), _resident((1, SSD_D_INNER)),
                  _resident((1, SSD_D_INNER)), _resident((LANES, SSD_D_INNER))],
        out_specs=[pl.BlockSpec((tm, GATES_WIDTH), lambda i: (cur(i), 0)),
                   pl.BlockSpec((QKV_WIDTH, tm), lambda i: (0, cur(i))),
                   pl.BlockSpec((tm, SSD_D_INNER), lambda i: (prev(i), 0))],
        out_shape=[jax.ShapeDtypeStruct((t, GATES_WIDTH), BF16), jax.ShapeDtypeStruct((QKV_WIDTH, t), BF16),
                   jax.ShapeDtypeStruct((t, SSD_D_INNER), BF16)],
        scratch_shapes=[pltpu.VMEM((CONV_HALO, SSD_CONV_DIM), F32), pltpu.VMEM((tm, IN_PROJ_TCHUNK), F32),
                        pltpu.VMEM((2, tm, SSD_D_INNER), BF16), pltpu.VMEM((2, tm, SSD_CONV_DIM), BF16),
                        pltpu.VMEM((2, tm, DT_PAD), F32),
                        pltpu.VMEM((SSD_N_GROUPS, SSD_D_STATE, SSD_GROUP_WIDTH), F32),
                        pltpu.VMEM((CHUNK, SSD_D_INNER), F32)],
        compiler_params=pltpu.CompilerParams(dimension_semantics=("arbitrary",), vmem_limit_bytes=VMEM_V7X),
        name="proj_ssd",
    )(x2, nw, w_zx, w_g, w_dt, w_qkv, cw, cb, dtb, alog, dexp, snw, e3)


MIX_NCHUNK = 256


def _attn_mix_kernel(sink_ref, qkvt_ref, pos_ref, invf_ref, yn_ref, gates_ref, x_ref, wso_ref, wao_ref, wmix_ref,
                     nw_ref, o_ref, qt, kk, vvt, ao_s, *, tiles_per_seq, n_tiles):
    L = CHUNK
    j = pl.program_id(0)
    tm = x_ref.shape[0]
    first_tile = (jnp.minimum(j, n_tiles - 1) % tiles_per_seq) == 0

    @pl.when(j == 0)
    def _():
        ao_s[...] = jnp.zeros(ao_s.shape, BF16)
        kk[...] = jnp.zeros(kk.shape, BF16)
        vvt[...] = jnp.zeros(vvt.shape, BF16)

    slot = j % 2
    cols = ATTN_REP * L
    key_i = lax.broadcasted_iota(jnp.int32, (L, cols), 0)
    qry_i = lax.broadcasted_iota(jnp.int32, (L, cols), 1) % L
    in_cur = key_i <= qry_i
    scale = ATTN_HEAD_DIM ** -0.5 * _LOG2E
    groups = range(ATTN_N_KV_HEADS)
    heads = [range(g * ATTN_REP, (g + 1) * ATTN_REP) for g in groups]
    sink = [jnp.concatenate([jnp.full((1, L), sink_ref[h] * _LOG2E, F32) for h in heads[g]], axis=1) for g in groups]

    def attn_stages(b):
        ts = slice(b * L, (b + 1) * L)
        pc, po = b % 2, 1 - b % 2
        ang = invf_ref[...] * pos_ref[:, ts].astype(F32)
        cosv = jnp.cos(ang)
        sinv = jnp.sin(ang)

        def rope(r0):
            t1 = qkvt_ref[r0:r0 + ROPE_HALF, ts].astype(F32)
            t2 = qkvt_ref[r0 + ROPE_HALF:r0 + ATTN_HEAD_DIM, ts].astype(F32)
            return jnp.concatenate([t1 * cosv - t2 * sinv, t2 * cosv + t1 * sinv], axis=0)

        for h in range(ATTN_N_HEADS):
            r0 = h * ATTN_HEAD_DIM
            qt[r0:r0 + ATTN_HEAD_DIM, :] = (rope(r0) * scale).astype(BF16)
            if h % 8 == 7:
                yield
        for g in groups:
            kk[pc, g] = rope(ATTN_WIDTH + g * ATTN_HEAD_DIM).T.astype(BF16)
        vvt[pc] = qkvt_ref[ATTN_WIDTH + KV_WIDTH:QKV_WIDTH, ts]
        yield
        old_bias = jnp.where(first_tile, -jnp.inf, 0.0) if b == 0 else 0.0
        s2 = [jnp.dot(jnp.concatenate([kk[pc, g], kk[po, g]], axis=0),
                      jnp.concatenate([qt[h * ATTN_HEAD_DIM:(h + 1) * ATTN_HEAD_DIM, :] for h in heads[g]], axis=1),
                      preferred_element_type=F32) for g in groups]
        yield
        s = [jnp.where(in_cur, s2[g][0:L], s2[g][L:2 * L] + old_bias) for g in groups]
        m = [jnp.maximum(jnp.max(s[g], axis=0, keepdims=True), sink[g]) for g in groups]
        yield
        p = [jnp.exp2(s[g] - m[g]) for g in groups]
        denom = [jnp.sum(p[g], axis=0, keepdims=True) + jnp.exp2(sink[g] - m[g]) for g in groups]
        yield
        pt = [jnp.concatenate([jnp.where(in_cur, p[g], 0.0), jnp.where(in_cur, 0.0, p[g])], axis=0).astype(BF16)
              for g in groups]
        yield
        vs = [slice(g * ATTN_HEAD_DIM, (g + 1) * ATTN_HEAD_DIM) for g in groups]
        o = [jnp.dot(jnp.concatenate([vvt[pc, vs[g], :], vvt[po, vs[g], :]], axis=1), pt[g],
                     preferred_element_type=F32) * (1.0 / denom[g]) for g in groups]
        for g in groups:
            for r, h in enumerate(heads[g]):
                ao_s[slot, b, h * ATTN_HEAD_DIM:(h + 1) * ATTN_HEAD_DIM, :] = o[g][:, r * L:(r + 1) * L].astype(BF16)
        yield

    def mix_stages():
        nc = D_MODEL // MIX_NCHUNK
        csl = [slice(c * MIX_NCHUNK, (c + 1) * MIX_NCHUNK) for c in range(nc)]
        yn = yn_ref[...]
        ys = []
        for c in range(nc):
            ys.append(jnp.dot(yn, wso_ref[:, csl[c]], preferred_element_type=F32))
            yield
        ya = jnp.concatenate([lax.dot_general(ao_s[1 - slot, b], wao_ref[...], _TN, preferred_element_type=F32)
                              for b in range(tm // L)], axis=0)
        yield
        merged = []
        for c in range(nc):
            gs = _sigmoid(gates_ref[:, csl[c]].astype(F32))
            ga = _sigmoid(gates_ref[:, D_MODEL + c * MIX_NCHUNK:D_MODEL + (c + 1) * MIX_NCHUNK].astype(F32))
            merged.append((gs * ys[c] + ga * ya[:, csl[c]]).astype(BF16))
        merged = jnp.concatenate(merged, axis=1)
        yield
        mo = []
        for c in range(nc):
            mo.append(jnp.dot(merged, wmix_ref[:, csl[c]], preferred_element_type=F32))
            yield
        mo = jnp.concatenate(mo, axis=1)
        ms = jnp.mean(mo * mo, axis=-1, keepdims=True)
        o_ref[...] = x_ref[...] + mo * lax.rsqrt(ms + NORM_EPS) * nw_ref[...]
        yield

    n_att, n_mix = (tm // L) * 8, 2 * (D_MODEL // MIX_NCHUNK) + 3
    _deal(mix_stages(), n_mix, (st for b in range(tm // L) for st in attn_stages(b)), n_att)


def _attn_mix(sinks, qkvt, pos_row, invf, yn, gates, x2, wso, wao, wmix, nw, tm, batch):
    t = x2.shape[0]
    n_tiles = t // tm
    cur = lambda h: pl.BlockSpec((h, tm), lambda j: (0, jnp.minimum(j, n_tiles - 1)))
    prev = lambda w: pl.BlockSpec((tm, w), lambda j: (jnp.maximum(j - 1, 0), 0))
    return pl.pallas_call(
        functools.partial(_attn_mix_kernel, tiles_per_seq=t // batch // tm, n_tiles=n_tiles),
        grid=(n_tiles + 1,),
        in_specs=[pl.BlockSpec(memory_space=pltpu.SMEM), cur(QKV_WIDTH), cur(1), _resident((ROPE_HALF, CHUNK)),
                  prev(SSD_D_INNER), prev(GATES_WIDTH), prev(D_MODEL),
                  _resident((SSD_D_INNER, D_MODEL)), _resident((ATTN_WIDTH, D_MODEL)),
                  _resident((D_MODEL, D_MODEL)), _resident((1, D_MODEL))],
        out_specs=prev(D_MODEL),
        out_shape=jax.ShapeDtypeStruct((t, D_MODEL), F32),
        scratch_shapes=[pltpu.VMEM((ATTN_WIDTH, CHUNK), BF16),
                        pltpu.VMEM((2, ATTN_N_KV_HEADS, CHUNK, ATTN_HEAD_DIM), BF16),
                        pltpu.VMEM((2, KV_WIDTH, CHUNK), BF16),
                        pltpu.VMEM((2, tm // CHUNK, ATTN_WIDTH, CHUNK), BF16)],
        compiler_params=pltpu.CompilerParams(dimension_semantics=("arbitrary",), vmem_limit_bytes=VMEM_LIMIT),
        name="attn_mix",
    )(sinks, qkvt, pos_row, invf, yn, gates, x2, wso, wao, wmix, nw)


FFN_NCHUNK = 256
FFN_SUBTILE = 256
_GELU_A = float(-2.0 * np.sqrt(2.0 / np.pi) * np.log2(np.e))
_GELU_B = float(_GELU_A * 0.044715)


def _gelu_tanh(v):
    return v / (1.0 + jnp.exp2(v * (_GELU_A + _GELU_B * (v * v))))


def _ffn_kernel(x_ref, npre_ref, wup_ref, cw_ref, cb_ref, wdn_ref, npost_ref, o_ref, carry, act,
                *, tiles_per_seq):
    tm = x_ref.shape[0]
    ts = min(FFN_SUBTILE, tm)

    @pl.when(pl.program_id(0) % tiles_per_seq == 0)
    def _():
        carry[...] = jnp.zeros(carry.shape, F32)

    def up(h, c0):
        return (jnp.dot(h, wup_ref[:, c0:c0 + FFN_NCHUNK], preferred_element_type=F32),
                jnp.dot(h, wup_ref[:, FFN_D_FF + c0:FFN_D_FF + c0 + FFN_NCHUNK], preferred_element_type=F32))

    def conv_chunk(r, c0):
        cs = slice(c0, c0 + FFN_NCHUNK)
        out = _causal_conv_rows(r, carry[:, cs], cw_ref[:, cs], cb_ref[:, cs])
        carry[:, cs] = r[ts - CONV_HALO:ts, :]
        return out

    chunks = list(range(0, FFN_D_FF, FFN_NCHUNK))
    for r0 in range(0, tm, ts):
        x = x_ref[r0:r0 + ts, :]
        ms = jnp.mean(x * x, axis=-1, keepdims=True)
        h = (x * lax.rsqrt(ms + NORM_EPS) * npre_ref[...]).astype(BF16)
        nxt = up(h, chunks[0])
        for i, c0 in enumerate(chunks):
            cur = nxt
            if i + 1 < len(chunks):
                nxt = up(h, chunks[i + 1])
            gate = conv_chunk(cur[0], c0).astype(BF16)
            val = conv_chunk(cur[1], FFN_D_FF + c0).astype(BF16)
            act[r0:r0 + ts, c0:c0 + FFN_NCHUNK] = _gelu_tanh(gate) * val
        ff = jnp.dot(act[r0:r0 + ts, :], wdn_ref[...], preferred_element_type=F32)
        ms2 = jnp.mean(ff * ff, axis=-1, keepdims=True)
        o_ref[r0:r0 + ts, :] = x + ff * lax.rsqrt(ms2 + NORM_EPS) * npost_ref[...]


def _ffn(x1, npre, wup, cw, cb, wdn, npost, tm, batch):
    t = x1.shape[0]
    row = lambda w: pl.BlockSpec((tm, w), lambda i: (i, 0))
    return pl.pallas_call(
        functools.partial(_ffn_kernel, tiles_per_seq=t // batch // tm),
        grid=(t // tm,),
        in_specs=[row(D_MODEL), _resident((1, D_MODEL)), _resident((D_MODEL, 2 * FFN_D_FF)),
                  _resident((FFN_CONV_WIDTH, 2 * FFN_D_FF)), _resident((1, 2 * FFN_D_FF)),
                  _resident((FFN_D_FF, D_MODEL)), _resident((1, D_MODEL))],
        out_specs=row(D_MODEL),
        out_shape=jax.ShapeDtypeStruct((t, D_MODEL), F32),
        scratch_shapes=[pltpu.VMEM((CONV_HALO, 2 * FFN_D_FF), F32),
                        pltpu.VMEM((tm, FFN_D_FF), BF16)],
        compiler_params=pltpu.CompilerParams(dimension_semantics=("arbitrary",), vmem_limit_bytes=VMEM_LIMIT),
        name="ffn",
    )(x1, npre, wup, cw, cb, wdn, npost)


def _expansion_matrix():
    e = np.zeros((LANES, SSD_D_INNER), np.float32)
    ch = np.arange(SSD_D_INNER)
    for part in range(3):
        e[part * SSD_N_HEADS + ch // SSD_HEAD_DIM, ch] = 1.0
    return jnp.asarray(e, dtype=BF16)


def _rope_inv_freq():
    inv = ROPE_THETA ** (-jnp.arange(ROPE_HALF, dtype=F32) * 2.0 / ATTN_HEAD_DIM)
    return jnp.broadcast_to(inv[:, None], (ROPE_HALF, CHUNK))


def _layer(x2, pos_row, batch, norm_mix_pre_w, w_in, ssd_conv_w, ssd_conv_b, ssd_dt_bias, ssd_a_log, ssd_d,
           ssd_norm_w, ssd_w_out, attn_sinks, attn_w_out, w_mix_out, norm_mix_post_w, norm_ffn_pre_w,
           ffn_w_up, ffn_conv_w, ffn_conv_b, ffn_w_down, norm_ffn_post_w):
    o = np.cumsum((0, SSD_D_INNER, SSD_CONV_DIM, SSD_N_HEADS, ATTN_WIDTH, KV_WIDTH, KV_WIDTH, D_MODEL, D_MODEL))
    w_zx = w_in[:, o[0]:o[2]].astype(BF16)
    w_g = w_in[:, o[6]:o[8]].astype(BF16)
    w_dt = jnp.pad(w_in[:, o[2]:o[3]], ((0, 0), (0, DT_PAD - SSD_N_HEADS))).astype(BF16)
    w_qkv = w_in[:, o[3]:o[6]].astype(BF16)
    row = lambda v: v.reshape(1, -1).astype(F32)
    pad_heads = lambda v: jnp.pad(v.astype(F32), (0, DT_PAD - SSD_N_HEADS)).reshape(1, DT_PAD)

    gates, qkvt, yn = _proj_ssd(x2, row(norm_mix_pre_w), w_zx, w_g, w_dt, w_qkv, ssd_conv_w.astype(F32),
                                row(ssd_conv_b), pad_heads(ssd_dt_bias), pad_heads(ssd_a_log),
                                row(jnp.repeat(ssd_d, SSD_HEAD_DIM)), row(ssd_norm_w), _expansion_matrix(), TILE, batch)
    x1 = _attn_mix(attn_sinks.astype(F32), qkvt, pos_row, _rope_inv_freq(), yn, gates, x2, ssd_w_out.astype(BF16),
                   attn_w_out.astype(BF16), w_mix_out.astype(BF16), row(norm_mix_post_w), TILE, batch)
    return _ffn(x1, row(norm_ffn_pre_w), ffn_w_up.astype(BF16), ffn_conv_w.astype(F32), row(ffn_conv_b),
                ffn_w_down.astype(BF16), row(norm_ffn_post_w), TILE, batch)


def kernel(x, positions, norm_mix_pre_w, w_in, ssd_conv_w, ssd_conv_b, ssd_dt_bias, ssd_a_log, ssd_d, ssd_norm_w,
           ssd_w_out, attn_sinks, attn_w_out, w_mix_out, norm_mix_post_w, norm_ffn_pre_w, ffn_w_up, ffn_conv_w,
           ffn_conv_b, ffn_w_down, norm_ffn_post_w):
    batch, seq, d = x.shape
    assert d == D_MODEL and seq % TILE == 0
    x2 = x.reshape(batch * seq, d)
    pos_row = positions.reshape(1, batch * seq)
    for i in range(w_in.shape[0]):
        x2 = _layer(x2, pos_row, batch, norm_mix_pre_w[i], w_in[i], ssd_conv_w[i], ssd_conv_b[i], ssd_dt_bias[i],
                    ssd_a_log[i], ssd_d[i], ssd_norm_w[i], ssd_w_out[i], attn_sinks[i], attn_w_out[i],
                    w_mix_out[i], norm_mix_post_w[i], norm_ffn_pre_w[i], ffn_w_up[i], ffn_conv_w[i],
                    ffn_conv_b[i], ffn_w_down[i], norm_ffn_post_w[i])
    return x2.reshape(batch, seq, d)
```

```python
import functools

import numpy as np
import jax
import jax.numpy as jnp
from jax import lax
from jax.experimental import pallas as pl
from jax.experimental.pallas import tpu as pltpu

F32 = jnp.float32
BF16 = jnp.bfloat16

D_MODEL = 1024
SSD_D_INNER = 2048
SSD_HEAD_DIM = 64
SSD_N_HEADS = 32
SSD_N_GROUPS = 4
SSD_HEADS_PER_GROUP = 8
SSD_D_STATE = 128
SSD_CONV_WIDTH = 4
SSD_CONV_DIM = 3072
SSD_GROUP_WIDTH = SSD_D_INNER // SSD_N_GROUPS
CHUNK = 128

ATTN_HEAD_DIM = 64
ATTN_N_HEADS = 16
ATTN_N_KV_HEADS = 4
ATTN_REP = 4
ATTN_WIDTH = 1024
KV_WIDTH = 256
QKV_WIDTH = ATTN_WIDTH + 2 * KV_WIDTH
ROPE_THETA = 10000.0
ROPE_HALF = ATTN_HEAD_DIM // 2

FFN_D_FF = 2816
FFN_CONV_WIDTH = 3
NORM_EPS = 1e-6

LANES = 128
SUBLANES = 8
DT_PAD = LANES
GATES_WIDTH = 2 * D_MODEL
TILE = ATTN_REP * CHUNK

VMEM_V7X = 64 * 1024 * 1024
VMEM_LIMIT = 56 * 1024 * 1024

_NT = (((1,), (1,)), ((), ()))
_TN = (((0,), (0,)), ((), ()))
_LOG2E = float(np.log2(np.e))
CONV_HALO = SUBLANES


def _resident(shape):
    nd = len(shape)
    return pl.BlockSpec(shape, lambda *_: (0,) * nd, pipeline_mode=pl.Buffered(1))


def _sigmoid(v):
    return 1.0 / (1.0 + jnp.exp(-v))


def _silu(v):
    return v / (1.0 + jnp.exp2(v * -_LOG2E))


def _split3(v):
    hi = v.astype(BF16).astype(F32)
    r = v - hi
    mid = r.astype(BF16).astype(F32)
    lo = (r - mid).astype(BF16).astype(F32)
    return hi, mid, lo


def _causal_conv_rows(r, halo, w, b):
    taps, (rows, width) = w.shape[0], r.shape
    nblk = rows // SUBLANES
    full = jnp.concatenate([halo, r], axis=0).reshape(nblk + 1, SUBLANES, width)
    sub = lax.broadcasted_iota(jnp.int32, (nblk, SUBLANES, width), 1)
    out = b + w[taps - 1:taps, :] * r
    for s in range(1, taps):
        merged = jnp.where(sub >= SUBLANES - s, full[0:nblk], full[1:nblk + 1])
        shifted = pltpu.roll(merged, s, 1).reshape(rows, width)
        out = out + w[taps - 1 - s:taps - s, :] * shifted
    return out


def _deal(major, n_major, minor, n_minor):
    done = 0
    for i in range(n_major):
        next(major)
        want = (i + 1) * n_minor // n_major
        while done < want:
            next(minor)
            done += 1
    assert next(major, "end") == "end" and next(minor, "end") == "end"


IN_PROJ_NCHUNK = 256
IN_PROJ_CONV_CHUNK = 256
IN_PROJ_TCHUNK = 256
_COL_B = SSD_D_INNER
_COL_C = SSD_D_INNER + SSD_N_GROUPS * SSD_D_STATE


def _proj_ssd_kernel(x_ref, nw_ref, wzx_ref, wg_ref, wdt_ref, wqkv_ref, cw_ref, cb_ref,
                     dtb_ref, alog_ref, dexp_ref, snw_ref, e3_ref,
                     gates_ref, qkvt_ref, yn_ref,
                     carry, tbuf, z_s, xbc_s, dt_s, state, yacc, *, tiles_per_seq, n_tiles):
    L = CHUNK
    j = pl.program_id(0)
    tm = x_ref.shape[0]
    slot = j % 2
    pslot = 1 - slot
    scan_tile = jnp.maximum(j - 1, 0)

    @pl.when(j == 0)
    def _():
        z_s[...] = jnp.zeros(z_s.shape, BF16)
        xbc_s[...] = jnp.zeros(xbc_s.shape, BF16)
        dt_s[...] = jnp.zeros(dt_s.shape, F32)
        state[...] = jnp.zeros(state.shape, F32)

    @pl.when(jnp.minimum(j, n_tiles - 1) % tiles_per_seq == 0)
    def _():
        carry[...] = jnp.zeros(carry.shape, F32)

    def proj_stages():
        x = x_ref[...]
        ms = jnp.mean(x * x, axis=-1, keepdims=True)
        u = (x * lax.rsqrt(ms + NORM_EPS) * nw_ref[...]).astype(BF16)
        yield

        def plain(store, w_ref, c, step):
            store(c, step, jnp.dot(u, w_ref[:, c:c + step], preferred_element_type=F32))

        def store_z(c, step, r):
            z_s[slot, :, c:c + step] = r.astype(BF16)

        def store_gates(c, step, r):
            gates_ref[:, c:c + step] = r.astype(BF16)

        def store_dt(c, step, r):
            dt_s[slot, :, c:c + step] = r

        def transposed(r0):
            tbuf[...] = jnp.dot(u, wqkv_ref[:, r0:r0 + IN_PROJ_TCHUNK], preferred_element_type=F32)
            qkvt_ref[r0:r0 + IN_PROJ_TCHUNK, :] = tbuf[...].T.astype(BF16)

        fillers = [functools.partial(plain, store, w_ref, c, min(IN_PROJ_NCHUNK, width))
                   for store, w_ref, width in ((store_z, wzx_ref, SSD_D_INNER), (store_gates, wg_ref, GATES_WIDTH),
                                               (store_dt, wdt_ref, DT_PAD))
                   for c in range(0, width, min(IN_PROJ_NCHUNK, width))]
        fillers += [functools.partial(transposed, r0) for r0 in range(0, QKV_WIDTH, IN_PROJ_TCHUNK)]
        n_conv = SSD_CONV_DIM // IN_PROJ_CONV_CHUNK
        per_conv = -(-len(fillers) // n_conv)
        for i in range(n_conv):
            c = i * IN_PROJ_CONV_CHUNK
            cs = slice(c, c + IN_PROJ_CONV_CHUNK)
            r = jnp.dot(u, wzx_ref[:, SSD_D_INNER + c:SSD_D_INNER + c + IN_PROJ_CONV_CHUNK],
                        preferred_element_type=F32)
            yield
            for f in fillers[i * per_conv:(i + 1) * per_conv]:
                f()
                yield
            acc = _causal_conv_rows(r, carry[:, cs], cw_ref[:, cs], cb_ref[:, cs])
            carry[:, cs] = r[tm - CONV_HALO:tm, :]
            xbc_s[slot, :, cs] = _silu(acc).astype(BF16)
            yield

    def scan_stages(c):
        rs = slice(c * L, (c + 1) * L)
        lane = lax.broadcasted_iota(jnp.int32, (L, LANES), 1)
        head_lane = lane < SSD_N_HEADS
        dtr = dt_s[pslot, rs, :] + dtb_ref[...]
        dt = jnp.maximum(dtr, 0.0) + jnp.log(1.0 + jnp.exp(-jnp.abs(dtr)))
        dt = jnp.where(head_lane, dt, 0.0)
        adt = dt * (-jnp.exp(alog_ref[...]))
        row_i = lax.broadcasted_iota(jnp.int32, (L, L), 0)
        col_i = lax.broadcasted_iota(jnp.int32, (L, L), 1)
        causal = col_i <= row_i
        tril = jnp.where(causal, 1.0, 0.0).astype(BF16)
        acs = sum(jnp.dot(tril, p.astype(BF16), preferred_element_type=F32) for p in _split3(adt)) * _LOG2E
        acs_dt_t = (acs - jnp.log2(dt)).T
        yield

        def pack3(v):
            hi, mid, lo = _split3(v)
            return jnp.where(head_lane, hi, jnp.where(lane < 2 * SSD_N_HEADS, pltpu.roll(mid, SSD_N_HEADS, 1),
                                                      pltpu.roll(lo, 2 * SSD_N_HEADS, 1))).astype(BF16)

        e_h = jnp.exp2(acs)
        w_h = dt * jnp.exp2(acs[L - 1:L, :] - acs)
        exp_in = jnp.concatenate([pack3(e_h), pack3(w_h)], axis=0)
        expd = jnp.dot(exp_in, e3_ref[...], preferred_element_type=F32)
        e_x = expd[0:L, :]
        w_x = expd[L:2 * L, :]
        cd_x = e_x[L - 1:L, :]
        yield
        reset = (scan_tile % tiles_per_seq) == 0
        for g in range(SSD_N_GROUPS):
            gs = slice(g * SSD_GROUP_WIDTH, (g + 1) * SSD_GROUP_WIDTH)
            xb = xbc_s[pslot, rs, gs]
            xg = xb.astype(F32)
            bm = xbc_s[pslot, rs, _COL_B + g * SSD_D_STATE:_COL_B + (g + 1) * SSD_D_STATE]
            cm = xbc_s[pslot, rs, _COL_C + g * SSD_D_STATE:_COL_C + (g + 1) * SSD_D_STATE]
            st = state[g]
            if c == 0:
                st = jnp.where(reset, 0.0, st)
            yoff = jnp.dot(cm, st.astype(BF16), preferred_element_type=F32) * e_x[:, gs]
            xd = (xg * w_x[:, gs]).astype(BF16)
            upd = lax.dot_general(bm, xd, _TN, preferred_element_type=F32)
            state[g] = st * cd_x[:, gs] + upd
            yacc[:, gs] = yoff + dexp_ref[:, gs] * xg
            cbm = lax.dot_general(cm, bm, _NT, preferred_element_type=F32).astype(BF16)
            yield
            for jp in range(0, SSD_HEADS_PER_GROUP, 2):
                pair = []
                for jh in (jp, jp + 1):
                    h = g * SSD_HEADS_PER_GROUP + jh
                    seg = acs[:, h:h + 1] - acs_dt_t[h:h + 1, :]
                    m = jnp.where(causal, jnp.exp2(seg), 0.0).astype(BF16) * cbm
                    pair.append(jnp.dot(m, xb[:, jh * SSD_HEAD_DIM:(jh + 1) * SSD_HEAD_DIM],
                                        preferred_element_type=F32))
                c0 = g * SSD_GROUP_WIDTH + jp * SSD_HEAD_DIM
                yacc[:, c0:c0 + LANES] += jnp.concatenate(pair, axis=1)
                yield
        for g in range(SSD_N_GROUPS):
            gs = slice(g * SSD_GROUP_WIDTH, (g + 1) * SSD_GROUP_WIDTH)
            gv = yacc[:, gs] * _silu(z_s[pslot, rs, gs].astype(F32))
            ms = jnp.mean(gv * gv, axis=-1, keepdims=True)
            yn_ref[rs, gs] = (gv * lax.rsqrt(ms + NORM_EPS) * snw_ref[:, gs]).astype(BF16)
            yield

    n_proj = 1 + (SSD_CONV_DIM // IN_PROJ_CONV_CHUNK) * 2 + len(range(0, SSD_D_INNER, IN_PROJ_NCHUNK)) \
        + len(range(0, GATES_WIDTH, IN_PROJ_NCHUNK)) + 1 + len(range(0, QKV_WIDTH, IN_PROJ_TCHUNK))
    n_scan = (tm // L) * (2 + SSD_N_GROUPS * (1 + SSD_HEADS_PER_GROUP // 2) + SSD_N_GROUPS)
    _deal(proj_stages(), n_proj, (st for c in range(tm // L) for st in scan_stages(c)), n_scan)


def _proj_ssd(x2, nw, w_zx, w_g, w_dt, w_qkv, cw, cb, dtb, alog, dexp, snw, e3, tm, batch):
    t = x2.shape[0]
    n_tiles = t // tm
    cur = lambda i: jnp.minimum(i, n_tiles - 1)
    prev = lambda i: jnp.maximum(i - 1, 0)
    return pl.pallas_call(
        functools.partial(_proj_ssd_kernel, tiles_per_seq=t // batch // tm, n_tiles=n_tiles),
        grid=(n_tiles + 1,),
        in_specs=[pl.BlockSpec((tm, D_MODEL), lambda i: (cur(i), 0)), _resident((1, D_MODEL)),
                  _resident((D_MODEL, SSD_D_INNER + SSD_CONV_DIM)), _resident((D_MODEL, GATES_WIDTH)),
                  _resident((D_MODEL, DT_PAD)), _resident((D_MODEL, QKV_WIDTH)),
                  _resident((SSD_CONV_WIDTH, SSD_CONV_DIM)), _resident((1, SSD_CONV_DIM)),
                  _resident((1, DT_PAD)), _resident((1, DT_PAD)), _resident((1, SSD_D_INNER)),
                  _resident((1, SSD_D_INNER)), _resident((LANES, SSD_D_INNER))],
        out_specs=[pl.BlockSpec((tm, GATES_WIDTH), lambda i: (cur(i), 0)),
                   pl.BlockSpec((QKV_WIDTH, tm), lambda i: (0, cur(i))),
                   pl.BlockSpec((tm, SSD_D_INNER), lambda i: (prev(i), 0))],
        out_shape=[jax.ShapeDtypeStruct((t, GATES_WIDTH), BF16), jax.ShapeDtypeStruct((QKV_WIDTH, t), BF16),
                   jax.ShapeDtypeStruct((t, SSD_D_INNER), BF16)],
        scratch_shapes=[pltpu.VMEM((CONV_HALO, SSD_CONV_DIM), F32), pltpu.VMEM((tm, IN_PROJ_TCHUNK), F32),
                        pltpu.VMEM((2, tm, SSD_D_INNER), BF16), pltpu.VMEM((2, tm, SSD_CONV_DIM), BF16),
                        pltpu.VMEM((2, tm, DT_PAD), F32),
                        pltpu.VMEM((SSD_N_GROUPS, SSD_D_STATE, SSD_GROUP_WIDTH), F32),
                        pltpu.VMEM((CHUNK, SSD_D_INNER), F32)],
        compiler_params=pltpu.CompilerParams(dimension_semantics=("arbitrary",), vmem_limit_bytes=VMEM_V7X),
        name="proj_ssd",
    )(x2, nw, w_zx, w_g, w_dt, w_qkv, cw, cb, dtb, alog, dexp, snw, e3)


MIX_NCHUNK = 256


def _attn_mix_kernel(sink_ref, qkvt_ref, pos_ref, invf_ref, yn_ref, gates_ref, x_ref, wso_ref, wao_ref, wmix_ref,
                     nw_ref, o_ref, qt, kk, vvt, ao_s, *, tiles_per_seq, n_tiles):
    L = CHUNK
    j = pl.program_id(0)
    tm = x_ref.shape[0]
    first_tile = (jnp.minimum(j, n_tiles - 1) % tiles_per_seq) == 0

    @pl.when(j == 0)
    def _():
        ao_s[...] = jnp.zeros(ao_s.shape, BF16)
        kk[...] = jnp.zeros(kk.shape, BF16)
        vvt[...] = jnp.zeros(vvt.shape, BF16)

    slot = j % 2
    cols = ATTN_REP * L
    key_i = lax.broadcasted_iota(jnp.int32, (L, cols), 0)
    qry_i = lax.broadcasted_iota(jnp.int32, (L, cols), 1) % L
    in_cur = key_i <= qry_i
    scale = ATTN_HEAD_DIM ** -0.5 * _LOG2E
    groups = range(ATTN_N_KV_HEADS)
    heads = [range(g * ATTN_REP, (g + 1) * ATTN_REP) for g in groups]
    sink = [jnp.concatenate([jnp.full((1, L), sink_ref[h] * _LOG2E, F32) for h in heads[g]], axis=1) for g in groups]

    def attn_stages(b):
        ts = slice(b * L, (b + 1) * L)
        pc, po = b % 2, 1 - b % 2
        ang = invf_ref[...] * pos_ref[:, ts].astype(F32)
        cosv = jnp.cos(ang)
        sinv = jnp.sin(ang)

        def rope(r0):
            t1 = qkvt_ref[r0:r0 + ROPE_HALF, ts].astype(F32)
            t2 = qkvt_ref[r0 + ROPE_HALF:r0 + ATTN_HEAD_DIM, ts].astype(F32)
            return jnp.concatenate([t1 * cosv - t2 * sinv, t2 * cosv + t1 * sinv], axis=0)

        for h in range(ATTN_N_HEADS):
            r0 = h * ATTN_HEAD_DIM
            qt[r0:r0 + ATTN_HEAD_DIM, :] = (rope(r0) * scale).astype(BF16)
            if h % 8 == 7:
                yield
        for g in groups:
            kk[pc, g] = rope(ATTN_WIDTH + g * ATTN_HEAD_DIM).T.astype(BF16)
        vvt[pc] = qkvt_ref[ATTN_WIDTH + KV_WIDTH:QKV_WIDTH, ts]
        yield
        old_bias = jnp.where(first_tile, -jnp.inf, 0.0) if b == 0 else 0.0
        s2 = [jnp.dot(jnp.concatenate([kk[pc, g], kk[po, g]], axis=0),
                      jnp.concatenate([qt[h * ATTN_HEAD_DIM:(h + 1) * ATTN_HEAD_DIM, :] for h in heads[g]], axis=1),
                      preferred_element_type=F32) for g in groups]
        yield
        s = [jnp.where(in_cur, s2[g][0:L], s2[g][L:2 * L] + old_bias) for g in groups]
        m = [jnp.maximum(jnp.max(s[g], axis=0, keepdims=True), sink[g]) for g in groups]
        yield
        p = [jnp.exp2(s[g] - m[g]) for g in groups]
        denom = [jnp.sum(p[g], axis=0, keepdims=True) + jnp.exp2(sink[g] - m[g]) for g in groups]
        yield
        pt = [jnp.concatenate([jnp.where(in_cur, p[g], 0.0), jnp.where(in_cur, 0.0, p[g])], axis=0).astype(BF16)
              for g in groups]
        yield
        vs = [slice(g * ATTN_HEAD_DIM, (g + 1) * ATTN_HEAD_DIM) for g in groups]
        o = [jnp.dot(jnp.concatenate([vvt[pc, vs[g], :], vvt[po, vs[g], :]], axis=1), pt[g],
                     preferred_element_type=F32) * (1.0 / denom[g]) for g in groups]
        for g in groups:
            for r, h in enumerate(heads[g]):
                ao_s[slot, b, h * ATTN_HEAD_DIM:(h + 1) * ATTN_HEAD_DIM, :] = o[g][:, r * L:(r + 1) * L].astype(BF16)
        yield

    def mix_stages():
        nc = D_MODEL // MIX_NCHUNK
        csl = [slice(c * MIX_NCHUNK, (c + 1) * MIX_NCHUNK) for c in range(nc)]
        yn = yn_ref[...]
        ys = []
        for c in range(nc):
            ys.append(jnp.dot(yn, wso_ref[:, csl[c]], preferred_element_type=F32))
            yield
        ya = jnp.concatenate([lax.dot_general(ao_s[1 - slot, b], wao_ref[...], _TN, preferred_element_type=F32)
                              for b in range(tm // L)], axis=0)
        yield
        merged = []
        for c in range(nc):
            gs = _sigmoid(gates_ref[:, csl[c]].astype(F32))
            ga = _sigmoid(gates_ref[:, D_MODEL + c * MIX_NCHUNK:D_MODEL + (c + 1) * MIX_NCHUNK].astype(F32))
            merged.append((gs * ys[c] + ga * ya[:, csl[c]]).astype(BF16))
        merged = jnp.concatenate(merged, axis=1)
        yield
        mo = []
        for c in range(nc):
            mo.append(jnp.dot(merged, wmix_ref[:, csl[c]], preferred_element_type=F32))
            yield
        mo = jnp.concatenate(mo, axis=1)
        ms = jnp.mean(mo * mo, axis=-1, keepdims=True)
        o_ref[...] = x_ref[...] + mo * lax.rsqrt(ms + NORM_EPS) * nw_ref[...]
        yield

    n_att, n_mix = (tm // L) * 8, 2 * (D_MODEL // MIX_NCHUNK) + 3
    _deal(mix_stages(), n_mix, (st for b in range(tm // L) for st in attn_stages(b)), n_att)


def _attn_mix(sinks, qkvt, pos_row, invf, yn, gates, x2, wso, wao, wmix, nw, tm, batch):
    t = x2.shape[0]
    n_tiles = t // tm
    cur = lambda h: pl.BlockSpec((h, tm), lambda j: (0, jnp.minimum(j, n_tiles - 1)))
    prev = lambda w: pl.BlockSpec((tm, w), lambda j: (jnp.maximum(j - 1, 0), 0))
    return pl.pallas_call(
        functools.partial(_attn_mix_kernel, tiles_per_seq=t // batch // tm, n_tiles=n_tiles),
        grid=(n_tiles + 1,),
        in_specs=[pl.BlockSpec(memory_space=pltpu.SMEM), cur(QKV_WIDTH), cur(1), _resident((ROPE_HALF, CHUNK)),
                  prev(SSD_D_INNER), prev(GATES_WIDTH), prev(D_MODEL),
                  _resident((SSD_D_INNER, D_MODEL)), _resident((ATTN_WIDTH, D_MODEL)),
                  _resident((D_MODEL, D_MODEL)), _resident((1, D_MODEL))],
        out_specs=prev(D_MODEL),
        out_shape=jax.ShapeDtypeStruct((t, D_MODEL), F32),
        scratch_shapes=[pltpu.VMEM((ATTN_WIDTH, CHUNK), BF16),
                        pltpu.VMEM((2, ATTN_N_KV_HEADS, CHUNK, ATTN_HEAD_DIM), BF16),
                        pltpu.VMEM((2, KV_WIDTH, CHUNK), BF16),
                        pltpu.VMEM((2, tm // CHUNK, ATTN_WIDTH, CHUNK), BF16)],
        compiler_params=pltpu.CompilerParams(dimension_semantics=("arbitrary",), vmem_limit_bytes=VMEM_LIMIT),
        name="attn_mix",
    )(sinks, qkvt, pos_row, invf, yn, gates, x2, wso, wao, wmix, nw)


FFN_NCHUNK = 256
FFN_SUBTILE = 256
_GELU_A = float(-2.0 * np.sqrt(2.0 / np.pi) * np.log2(np.e))
_GELU_B = float(_GELU_A * 0.044715)


def _gelu_tanh(v):
    return v / (1.0 + jnp.exp2(v * (_GELU_A + _GELU_B * (v * v))))


def _ffn_kernel(x_ref, npre_ref, wup_ref, cw_ref, cb_ref, wdn_ref, npost_ref, o_ref, carry, act,
                *, tiles_per_seq):
    tm = x_ref.shape[0]
    ts = min(FFN_SUBTILE, tm)

    @pl.when(pl.program_id(0) % tiles_per_seq == 0)
    def _():
        carry[...] = jnp.zeros(carry.shape, F32)

    def up(h, c0):
        return (jnp.dot(h, wup_ref[:, c0:c0 + FFN_NCHUNK], preferred_element_type=F32),
                jnp.dot(h, wup_ref[:, FFN_D_FF + c0:FFN_D_FF + c0 + FFN_NCHUNK], preferred_element_type=F32))

    def conv_chunk(r, c0):
        cs = slice(c0, c0 + FFN_NCHUNK)
        out = _causal_conv_rows(r, carry[:, cs], cw_ref[:, cs], cb_ref[:, cs])
        carry[:, cs] = r[ts - CONV_HALO:ts, :]
        return out

    chunks = list(range(0, FFN_D_FF, FFN_NCHUNK))

    def up_stages(r0):
        x = x_ref[r0:r0 + ts, :]
        ms = jnp.mean(x * x, axis=-1, keepdims=True)
        h = (x * lax.rsqrt(ms + NORM_EPS) * npre_ref[...]).astype(BF16)
        nxt = up(h, chunks[0])
        for i, c0 in enumerate(chunks):
            cur = nxt
            if i + 1 < len(chunks):
                nxt = up(h, chunks[i + 1])
            gate = conv_chunk(cur[0], c0).astype(BF16)
            val = conv_chunk(cur[1], FFN_D_FF + c0).astype(BF16)
            act[r0:r0 + ts, c0:c0 + FFN_NCHUNK] = _gelu_tanh(gate) * val
            yield

    def down_stages(r0):
        a = act[r0:r0 + ts, :]
        ff = []
        for c in range(0, D_MODEL, FFN_NCHUNK):
            ff.append(jnp.dot(a, wdn_ref[:, c:c + FFN_NCHUNK], preferred_element_type=F32))
            yield
        ff = jnp.concatenate(ff, axis=1)
        ms2 = jnp.mean(ff * ff, axis=-1, keepdims=True)
        o_ref[r0:r0 + ts, :] = x_ref[r0:r0 + ts, :] + ff * lax.rsqrt(ms2 + NORM_EPS) * npost_ref[...]
        yield

    n_down = D_MODEL // FFN_NCHUNK + 1
    starts = list(range(0, tm, ts))
    for _ in up_stages(starts[0]):
        pass
    for prev_r0, r0 in zip(starts[:-1], starts[1:]):
        _deal(up_stages(r0), len(chunks), down_stages(prev_r0), n_down)
    for _ in down_stages(starts[-1]):
        pass


def _ffn(x1, npre, wup, cw, cb, wdn, npost, tm, batch):
    t = x1.shape[0]
    row = lambda w: pl.BlockSpec((tm, w), lambda i: (i, 0))
    return pl.pallas_call(
        functools.partial(_ffn_kernel, tiles_per_seq=t // batch // tm),
        grid=(t // tm,),
        in_specs=[row(D_MODEL), _resident((1, D_MODEL)), _resident((D_MODEL, 2 * FFN_D_FF)),
                  _resident((FFN_CONV_WIDTH, 2 * FFN_D_FF)), _resident((1, 2 * FFN_D_FF)),
                  _resident((FFN_D_FF, D_MODEL)), _resident((1, D_MODEL))],
        out_specs=row(D_MODEL),
        out_shape=jax.ShapeDtypeStruct((t, D_MODEL), F32),
        scratch_shapes=[pltpu.VMEM((CONV_HALO, 2 * FFN_D_FF), F32),
                        pltpu.VMEM((tm, FFN_D_FF), BF16)],
        compiler_params=pltpu.CompilerParams(dimension_semantics=("arbitrary",), vmem_limit_bytes=VMEM_LIMIT),
        name="ffn",
    )(x1, npre, wup, cw, cb, wdn, npost)


def _expansion_matrix():
    e = np.zeros((LANES, SSD_D_INNER), np.float32)
    ch = np.arange(SSD_D_INNER)
    for part in range(3):
        e[part * SSD_N_HEADS + ch // SSD_HEAD_DIM, ch] = 1.0
    return jnp.asarray(e, dtype=BF16)


def _rope_inv_freq():
    inv = ROPE_THETA ** (-jnp.arange(ROPE_HALF, dtype=F32) * 2.0 / ATTN_HEAD_DIM)
    return jnp.broadcast_to(inv[:, None], (ROPE_HALF, CHUNK))


def _layer(x2, pos_row, batch, norm_mix_pre_w, w_in, ssd_conv_w, ssd_conv_b, ssd_dt_bias, ssd_a_log, ssd_d,
           ssd_norm_w, ssd_w_out, attn_sinks, attn_w_out, w_mix_out, norm_mix_post_w, norm_ffn_pre_w,
           ffn_w_up, ffn_conv_w, ffn_conv_b, ffn_w_down, norm_ffn_post_w):
    o = np.cumsum((0, SSD_D_INNER, SSD_CONV_DIM, SSD_N_HEADS, ATTN_WIDTH, KV_WIDTH, KV_WIDTH, D_MODEL, D_MODEL))
    w_zx = w_in[:, o[0]:o[2]].astype(BF16)
    w_g = w_in[:, o[6]:o[8]].astype(BF16)
    w_dt = jnp.pad(w_in[:, o[2]:o[3]], ((0, 0), (0, DT_PAD - SSD_N_HEADS))).astype(BF16)
    w_qkv = w_in[:, o[3]:o[6]].astype(BF16)
    row = lambda v: v.reshape(1, -1).astype(F32)
    pad_heads = lambda v: jnp.pad(v.astype(F32), (0, DT_PAD - SSD_N_HEADS)).reshape(1, DT_PAD)

    gates, qkvt, yn = _proj_ssd(x2, row(norm_mix_pre_w), w_zx, w_g, w_dt, w_qkv, ssd_conv_w.astype(F32),
                                row(ssd_conv_b), pad_heads(ssd_dt_bias), pad_heads(ssd_a_log),
                                row(jnp.repeat(ssd_d, SSD_HEAD_DIM)), row(ssd_norm_w), _expansion_matrix(), TILE, batch)
    x1 = _attn_mix(attn_sinks.astype(F32), qkvt, pos_row, _rope_inv_freq(), yn, gates, x2, ssd_w_out.astype(BF16),
                   attn_w_out.astype(BF16), w_mix_out.astype(BF16), row(norm_mix_post_w), TILE, batch)
    return _ffn(x1, row(norm_ffn_pre_w), ffn_w_up.astype(BF16), ffn_conv_w.astype(F32), row(ffn_conv_b),
                ffn_w_down.astype(BF16), row(norm_ffn_post_w), TILE, batch)


def kernel(x, positions, norm_mix_pre_w, w_in, ssd_conv_w, ssd_conv_b, ssd_dt_bias, ssd_a_log, ssd_d, ssd_norm_w,
           ssd_w_out, attn_sinks, attn_w_out, w_mix_out, norm_mix_post_w, norm_ffn_pre_w, ffn_w_up, ffn_conv_w,
           ffn_conv_b, ffn_w_down, norm_ffn_post_w):
    batch, seq, d = x.shape
    assert d == D_MODEL and seq % TILE == 0
    x2 = x.reshape(batch * seq, d)
    pos_row = positions.reshape(1, batch * seq)
    for i in range(w_in.shape[0]):
        x2 = _layer(x2, pos_row, batch, norm_mix_pre_w[i], w_in[i], ssd_conv_w[i], ssd_conv_b[i], ssd_dt_bias[i],
                    ssd_a_log[i], ssd_d[i], ssd_norm_w[i], ssd_w_out[i], attn_sinks[i], attn_w_out[i],
                    w_mix_out[i], norm_mix_post_w[i], norm_ffn_pre_w[i], ffn_w_up[i], ffn_conv_w[i],
                    ffn_conv_b[i], ffn_w_down[i], norm_ffn_post_w[i])
    return x2.reshape(batch, seq, d)
```

```python
import functools

import numpy as np
import jax
import jax.numpy as jnp
from jax import lax
from jax.experimental import pallas as pl
from jax.experimental.pallas import tpu as pltpu

F32 = jnp.float32
BF16 = jnp.bfloat16

D_MODEL = 1024
SSD_D_INNER = 2048
SSD_HEAD_DIM = 64
SSD_N_HEADS = 32
SSD_N_GROUPS = 4
SSD_HEADS_PER_GROUP = 8
SSD_D_STATE = 128
SSD_CONV_WIDTH = 4
SSD_CONV_DIM = 3072
SSD_GROUP_WIDTH = SSD_D_INNER // SSD_N_GROUPS
CHUNK = 128

ATTN_HEAD_DIM = 64
ATTN_N_HEADS = 16
ATTN_N_KV_HEADS = 4
ATTN_REP = 4
ATTN_WIDTH = 1024
KV_WIDTH = 256
QKV_WIDTH = ATTN_WIDTH + 2 * KV_WIDTH
ROPE_THETA = 10000.0
ROPE_HALF = ATTN_HEAD_DIM // 2

FFN_D_FF = 2816
FFN_CONV_WIDTH = 3
NORM_EPS = 1e-6

LANES = 128
SUBLANES = 8
DT_PAD = LANES
GATES_WIDTH = 2 * D_MODEL
TILE = ATTN_REP * CHUNK

VMEM_V7X = 64 * 1024 * 1024
VMEM_LIMIT = 56 * 1024 * 1024

_NT = (((1,), (1,)), ((), ()))
_TN = (((0,), (0,)), ((), ()))
_LOG2E = float(np.log2(np.e))
CONV_HALO = SUBLANES


def _resident(shape):
    nd = len(shape)
    return pl.BlockSpec(shape, lambda *_: (0,) * nd, pipeline_mode=pl.Buffered(1))


def _sigmoid(v):
    return 1.0 / (1.0 + jnp.exp(-v))


def _silu(v):
    return v / (1.0 + jnp.exp2(v * -_LOG2E))


def _split3(v):
    hi = v.astype(BF16).astype(F32)
    r = v - hi
    mid = r.astype(BF16).astype(F32)
    lo = (r - mid).astype(BF16).astype(F32)
    return hi, mid, lo


def _causal_conv_rows(r, halo, w, b):
    taps, (rows, width) = w.shape[0], r.shape
    nblk = rows // SUBLANES
    full = jnp.concatenate([halo, r], axis=0).reshape(nblk + 1, SUBLANES, width)
    sub = lax.broadcasted_iota(jnp.int32, (nblk, SUBLANES, width), 1)
    out = b + w[taps - 1:taps, :] * r
    for s in range(1, taps):
        merged = jnp.where(sub >= SUBLANES - s, full[0:nblk], full[1:nblk + 1])
        shifted = pltpu.roll(merged, s, 1).reshape(rows, width)
        out = out + w[taps - 1 - s:taps - s, :] * shifted
    return out


def _deal(major, n_major, minor, n_minor):
    done = 0
    for i in range(n_major):
        next(major)
        want = (i + 1) * n_minor // n_major
        while done < want:
            next(minor)
            done += 1
    assert next(major, "end") == "end" and next(minor, "end") == "end"


IN_PROJ_NCHUNK = 256
IN_PROJ_CONV_CHUNK = 256
IN_PROJ_TCHUNK = 256
_COL_B = SSD_D_INNER
_COL_C = SSD_D_INNER + SSD_N_GROUPS * SSD_D_STATE


def _proj_ssd_kernel(x_ref, nw_ref, wzx_ref, wg_ref, wdt_ref, wqkv_ref, cw_ref, cb_ref,
                     dtb_ref, alog_ref, dexp_ref, snw_ref, e3_ref,
                     gates_ref, qkvt_ref, yn_ref,
                     carry, tbuf, z_s, xbc_s, dt_s, state, yacc, *, tiles_per_seq, n_tiles):
    L = CHUNK
    j = pl.program_id(0)
    tm = x_ref.shape[0]
    slot = j % 2
    pslot = 1 - slot
    scan_tile = jnp.maximum(j - 1, 0)

    @pl.when(j == 0)
    def _():
        z_s[...] = jnp.zeros(z_s.shape, BF16)
        xbc_s[...] = jnp.zeros(xbc_s.shape, BF16)
        dt_s[...] = jnp.zeros(dt_s.shape, F32)
        state[...] = jnp.zeros(state.shape, F32)

    @pl.when(jnp.minimum(j, n_tiles - 1) % tiles_per_seq == 0)
    def _():
        carry[...] = jnp.zeros(carry.shape, F32)

    def proj_stages():
        x = x_ref[...]
        ms = jnp.mean(x * x, axis=-1, keepdims=True)
        u = (x * lax.rsqrt(ms + NORM_EPS) * nw_ref[...]).astype(BF16)
        yield

        def plain(store, w_ref, c, step):
            store(c, step, jnp.dot(u, w_ref[:, c:c + step], preferred_element_type=F32))

        def store_z(c, step, r):
            z_s[slot, :, c:c + step] = r.astype(BF16)

        def store_gates(c, step, r):
            gates_ref[:, c:c + step] = r.astype(BF16)

        def store_dt(c, step, r):
            dt_s[slot, :, c:c + step] = r

        def transposed(r0):
            tbuf[...] = jnp.dot(u, wqkv_ref[:, r0:r0 + IN_PROJ_TCHUNK], preferred_element_type=F32)
            qkvt_ref[r0:r0 + IN_PROJ_TCHUNK, :] = tbuf[...].T.astype(BF16)

        fillers = [functools.partial(plain, store, w_ref, c, min(IN_PROJ_NCHUNK, width))
                   for store, w_ref, width in ((store_z, wzx_ref, SSD_D_INNER), (store_gates, wg_ref, GATES_WIDTH),
                                               (store_dt, wdt_ref, DT_PAD))
                   for c in range(0, width, min(IN_PROJ_NCHUNK, width))]
        fillers += [functools.partial(transposed, r0) for r0 in range(0, QKV_WIDTH, IN_PROJ_TCHUNK)]
        n_conv = SSD_CONV_DIM // IN_PROJ_CONV_CHUNK
        per_conv = -(-len(fillers) // n_conv)
        for i in range(n_conv):
            c = i * IN_PROJ_CONV_CHUNK
            cs = slice(c, c + IN_PROJ_CONV_CHUNK)
            r = jnp.dot(u, wzx_ref[:, SSD_D_INNER + c:SSD_D_INNER + c + IN_PROJ_CONV_CHUNK],
                        preferred_element_type=F32)
            yield
            for f in fillers[i * per_conv:(i + 1) * per_conv]:
                f()
                yield
            acc = _causal_conv_rows(r, carry[:, cs], cw_ref[:, cs], cb_ref[:, cs])
            carry[:, cs] = r[tm - CONV_HALO:tm, :]
            xbc_s[slot, :, cs] = _silu(acc).astype(BF16)
            yield

    def scan_stages(c):
        rs = slice(c * L, (c + 1) * L)
        lane = lax.broadcasted_iota(jnp.int32, (L, LANES), 1)
        head_lane = lane < SSD_N_HEADS
        dtr = dt_s[pslot, rs, :] + dtb_ref[...]
        dt = jnp.maximum(dtr, 0.0) + jnp.log(1.0 + jnp.exp(-jnp.abs(dtr)))
        dt = jnp.where(head_lane, dt, 0.0)
        adt = dt * (-jnp.exp(alog_ref[...]))
        row_i = lax.broadcasted_iota(jnp.int32, (L, L), 0)
        col_i = lax.broadcasted_iota(jnp.int32, (L, L), 1)
        causal = col_i <= row_i
        tril = jnp.where(causal, 1.0, 0.0).astype(BF16)
        acs = sum(jnp.dot(tril, p.astype(BF16), preferred_element_type=F32) for p in _split3(adt)) * _LOG2E
        acs_dt_t = (acs - jnp.log2(dt)).T
        yield

        def pack3(v):
            hi, mid, lo = _split3(v)
            return jnp.where(head_lane, hi, jnp.where(lane < 2 * SSD_N_HEADS, pltpu.roll(mid, SSD_N_HEADS, 1),
                                                      pltpu.roll(lo, 2 * SSD_N_HEADS, 1))).astype(BF16)

        e_h = jnp.exp2(acs)
        w_h = dt * jnp.exp2(acs[L - 1:L, :] - acs)
        exp_in = jnp.concatenate([pack3(e_h), pack3(w_h)], axis=0)
        expd = jnp.dot(exp_in, e3_ref[...], preferred_element_type=F32)
        e_x = expd[0:L, :]
        w_x = expd[L:2 * L, :]
        cd_x = e_x[L - 1:L, :]
        yield
        reset = (scan_tile % tiles_per_seq) == 0
        for g in range(SSD_N_GROUPS):
            gs = slice(g * SSD_GROUP_WIDTH, (g + 1) * SSD_GROUP_WIDTH)
            xb = xbc_s[pslot, rs, gs]
            xg = xb.astype(F32)
            bm = xbc_s[pslot, rs, _COL_B + g * SSD_D_STATE:_COL_B + (g + 1) * SSD_D_STATE]
            cm = xbc_s[pslot, rs, _COL_C + g * SSD_D_STATE:_COL_C + (g + 1) * SSD_D_STATE]
            st = state[g]
            if c == 0:
                st = jnp.where(reset, 0.0, st)
            yoff = jnp.dot(cm, st.astype(BF16), preferred_element_type=F32) * e_x[:, gs]
            xd = (xg * w_x[:, gs]).astype(BF16)
            upd = lax.dot_general(bm, xd, _TN, preferred_element_type=F32)
            state[g] = st * cd_x[:, gs] + upd
            yacc[:, gs] = yoff + dexp_ref[:, gs] * xg
            cbm = lax.dot_general(cm, bm, _NT, preferred_element_type=F32).astype(BF16)
            yield
            for jp in range(0, SSD_HEADS_PER_GROUP, 2):
                pair = []
                for jh in (jp, jp + 1):
                    h = g * SSD_HEADS_PER_GROUP + jh
                    seg = acs[:, h:h + 1] - acs_dt_t[h:h + 1, :]
                    m = jnp.where(causal, jnp.exp2(seg), 0.0).astype(BF16) * cbm
                    pair.append(jnp.dot(m, xb[:, jh * SSD_HEAD_DIM:(jh + 1) * SSD_HEAD_DIM],
                                        preferred_element_type=F32))
                c0 = g * SSD_GROUP_WIDTH + jp * SSD_HEAD_DIM
                yacc[:, c0:c0 + LANES] += jnp.concatenate(pair, axis=1)
                yield
        for g in range(SSD_N_GROUPS):
            gs = slice(g * SSD_GROUP_WIDTH, (g + 1) * SSD_GROUP_WIDTH)
            gv = yacc[:, gs] * _silu(z_s[pslot, rs, gs].astype(F32))
            ms = jnp.mean(gv * gv, axis=-1, keepdims=True)
            yn_ref[rs, gs] = (gv * lax.rsqrt(ms + NORM_EPS) * snw_ref[:, gs]).astype(BF16)
            yield

    n_proj = 1 + (SSD_CONV_DIM // IN_PROJ_CONV_CHUNK) * 2 + len(range(0, SSD_D_INNER, IN_PROJ_NCHUNK)) \
        + len(range(0, GATES_WIDTH, IN_PROJ_NCHUNK)) + 1 + len(range(0, QKV_WIDTH, IN_PROJ_TCHUNK))
    n_scan = (tm // L) * (2 + SSD_N_GROUPS * (1 + SSD_HEADS_PER_GROUP // 2) + SSD_N_GROUPS)
    _deal(proj_stages(), n_proj, (st for c in range(tm // L) for st in scan_stages(c)), n_scan)


def _proj_ssd(x2, nw, w_zx, w_g, w_dt, w_qkv, cw, cb, dtb, alog, dexp, snw, e3, tm, batch):
    t = x2.shape[0]
    n_tiles = t // tm
    cur = lambda i: jnp.minimum(i, n_tiles - 1)
    prev = lambda i: jnp.maximum(i - 1, 0)
    return pl.pallas_call(
        functools.partial(_proj_ssd_kernel, tiles_per_seq=t // batch // tm, n_tiles=n_tiles),
        grid=(n_tiles + 1,),
        in_specs=[pl.BlockSpec((tm, D_MODEL), lambda i: (cur(i), 0)), _resident((1, D_MODEL)),
                  _resident((D_MODEL, SSD_D_INNER + SSD_CONV_DIM)), _resident((D_MODEL, GATES_WIDTH)),
                  _resident((D_MODEL, DT_PAD)), _resident((D_MODEL, QKV_WIDTH)),
                  _resident((SSD_CONV_WIDTH, SSD_CONV_DIM)), _resident((1, SSD_CONV_DIM)),
                  _resident((1, DT_PAD)), _resident((1, DT_PAD)), _resident((1, SSD_D_INNER)),
                  _resident((1, SSD_D_INNER)), _resident((LANES, SSD_D_INNER))],
        out_specs=[pl.BlockSpec((tm, GATES_WIDTH), lambda i: (cur(i), 0)),
                   pl.BlockSpec((QKV_WIDTH, tm), lambda i: (0, cur(i))),
                   pl.BlockSpec((tm, SSD_D_INNER), lambda i: (prev(i), 0))],
        out_shape=[jax.ShapeDtypeStruct((t, GATES_WIDTH), BF16), jax.ShapeDtypeStruct((QKV_WIDTH, t), BF16),
                   jax.ShapeDtypeStruct((t, SSD_D_INNER), BF16)],
        scratch_shapes=[pltpu.VMEM((CONV_HALO, SSD_CONV_DIM), F32), pltpu.VMEM((tm, IN_PROJ_TCHUNK), F32),
                        pltpu.VMEM((2, tm, SSD_D_INNER), BF16), pltpu.VMEM((2, tm, SSD_CONV_DIM), BF16),
                        pltpu.VMEM((2, tm, DT_PAD), F32),
                        pltpu.VMEM((SSD_N_GROUPS, SSD_D_STATE, SSD_GROUP_WIDTH), F32),
                        pltpu.VMEM((CHUNK, SSD_D_INNER), F32)],
        compiler_params=pltpu.CompilerParams(dimension_semantics=("arbitrary",), vmem_limit_bytes=VMEM_V7X),
        name="proj_ssd",
    )(x2, nw, w_zx, w_g, w_dt, w_qkv, cw, cb, dtb, alog, dexp, snw, e3)


MIX_NCHUNK = 256


def _attn_mix_kernel(sink_ref, qkvt_ref, pos_ref, invf_ref, yn_ref, gates_ref, x_ref, wso_ref, wao_ref, wmix_ref,
                     nw_ref, o_ref, qt, kk, vvt, ao_s, *, tiles_per_seq, n_tiles):
    L = CHUNK
    j = pl.program_id(0)
    tm = x_ref.shape[0]
    first_tile = (jnp.minimum(j, n_tiles - 1) % tiles_per_seq) == 0

    @pl.when(j == 0)
    def _():
        ao_s[...] = jnp.zeros(ao_s.shape, BF16)
        kk[...] = jnp.zeros(kk.shape, BF16)
        vvt[...] = jnp.zeros(vvt.shape, BF16)

    slot = j % 2
    cols = ATTN_REP * L
    key_i = lax.broadcasted_iota(jnp.int32, (L, cols), 0)
    qry_i = lax.broadcasted_iota(jnp.int32, (L, cols), 1) % L
    in_cur = key_i <= qry_i
    scale = ATTN_HEAD_DIM ** -0.5 * _LOG2E
    groups = range(ATTN_N_KV_HEADS)
    heads = [range(g * ATTN_REP, (g + 1) * ATTN_REP) for g in groups]
    sink = [jnp.concatenate([jnp.full((1, L), sink_ref[h] * _LOG2E, F32) for h in heads[g]], axis=1) for g in groups]

    def attn_stages(b):
        ts = slice(b * L, (b + 1) * L)
        pc, po = b % 2, 1 - b % 2
        ang = invf_ref[...] * pos_ref[:, ts].astype(F32)
        cosv = jnp.cos(ang)
        sinv = jnp.sin(ang)

        def rope(r0):
            t1 = qkvt_ref[r0:r0 + ROPE_HALF, ts].astype(F32)
            t2 = qkvt_ref[r0 + ROPE_HALF:r0 + ATTN_HEAD_DIM, ts].astype(F32)
            return jnp.concatenate([t1 * cosv - t2 * sinv, t2 * cosv + t1 * sinv], axis=0)

        for h in range(ATTN_N_HEADS):
            r0 = h * ATTN_HEAD_DIM
            qt[r0:r0 + ATTN_HEAD_DIM, :] = (rope(r0) * scale).astype(BF16)
            if h % 8 == 7:
                yield
        for g in groups:
            kk[pc, g] = rope(ATTN_WIDTH + g * ATTN_HEAD_DIM).T.astype(BF16)
        vvt[pc] = qkvt_ref[ATTN_WIDTH + KV_WIDTH:QKV_WIDTH, ts]
        yield
        old_bias = jnp.where(first_tile, -jnp.inf, 0.0) if b == 0 else 0.0
        s2 = [jnp.dot(jnp.concatenate([kk[pc, g], kk[po, g]], axis=0),
                      jnp.concatenate([qt[h * ATTN_HEAD_DIM:(h + 1) * ATTN_HEAD_DIM, :] for h in heads[g]], axis=1),
                      preferred_element_type=F32) for g in groups]
        yield
        s = [jnp.where(in_cur, s2[g][0:L], s2[g][L:2 * L] + old_bias) for g in groups]
        m = [jnp.maximum(jnp.max(s[g], axis=0, keepdims=True), sink[g]) for g in groups]
        yield
        p = [jnp.exp2(s[g] - m[g]) for g in groups]
        denom = [jnp.sum(p[g], axis=0, keepdims=True) + jnp.exp2(sink[g] - m[g]) for g in groups]
        yield
        pt = [jnp.concatenate([jnp.where(in_cur, p[g], 0.0), jnp.where(in_cur, 0.0, p[g])], axis=0).astype(BF16)
              for g in groups]
        yield
        vs = [slice(g * ATTN_HEAD_DIM, (g + 1) * ATTN_HEAD_DIM) for g in groups]
        o = [jnp.dot(jnp.concatenate([vvt[pc, vs[g], :], vvt[po, vs[g], :]], axis=1), pt[g],
                     preferred_element_type=F32) * (1.0 / denom[g]) for g in groups]
        for g in groups:
            for r, h in enumerate(heads[g]):
                ao_s[slot, b, h * ATTN_HEAD_DIM:(h + 1) * ATTN_HEAD_DIM, :] = o[g][:, r * L:(r + 1) * L].astype(BF16)
        yield

    def mix_stages():
        nc = D_MODEL // MIX_NCHUNK
        csl = [slice(c * MIX_NCHUNK, (c + 1) * MIX_NCHUNK) for c in range(nc)]
        yn = yn_ref[...]
        ys = []
        for c in range(nc):
            ys.append(jnp.dot(yn, wso_ref[:, csl[c]], preferred_element_type=F32))
            yield
        ya = jnp.concatenate([lax.dot_general(ao_s[1 - slot, b], wao_ref[...], _TN, preferred_element_type=F32)
                              for b in range(tm // L)], axis=0)
        yield
        merged = []
        for c in range(nc):
            gs = _sigmoid(gates_ref[:, csl[c]].astype(F32))
            ga = _sigmoid(gates_ref[:, D_MODEL + c * MIX_NCHUNK:D_MODEL + (c + 1) * MIX_NCHUNK].astype(F32))
            merged.append((gs * ys[c] + ga * ya[:, csl[c]]).astype(BF16))
        merged = jnp.concatenate(merged, axis=1)
        yield
        mo = []
        for c in range(nc):
            mo.append(jnp.dot(merged, wmix_ref[:, csl[c]], preferred_element_type=F32))
            yield
        mo = jnp.concatenate(mo, axis=1)
        ms = jnp.mean(mo * mo, axis=-1, keepdims=True)
        o_ref[...] = x_ref[...] + mo * lax.rsqrt(ms + NORM_EPS) * nw_ref[...]
        yield

    n_att, n_mix = (tm // L) * 8, 2 * (D_MODEL // MIX_NCHUNK) + 3
    _deal((st for b in range(tm // L) for st in attn_stages(b)), n_att, mix_stages(), n_mix)


def _attn_mix(sinks, qkvt, pos_row, invf, yn, gates, x2, wso, wao, wmix, nw, tm, batch):
    t = x2.shape[0]
    n_tiles = t // tm
    cur = lambda h: pl.BlockSpec((h, tm), lambda j: (0, jnp.minimum(j, n_tiles - 1)))
    prev = lambda w: pl.BlockSpec((tm, w), lambda j: (jnp.maximum(j - 1, 0), 0))
    return pl.pallas_call(
        functools.partial(_attn_mix_kernel, tiles_per_seq=t // batch // tm, n_tiles=n_tiles),
        grid=(n_tiles + 1,),
        in_specs=[pl.BlockSpec(memory_space=pltpu.SMEM), cur(QKV_WIDTH), cur(1), _resident((ROPE_HALF, CHUNK)),
                  prev(SSD_D_INNER), prev(GATES_WIDTH), prev(D_MODEL),
                  _resident((SSD_D_INNER, D_MODEL)), _resident((ATTN_WIDTH, D_MODEL)),
                  _resident((D_MODEL, D_MODEL)), _resident((1, D_MODEL))],
        out_specs=prev(D_MODEL),
        out_shape=jax.ShapeDtypeStruct((t, D_MODEL), F32),
        scratch_shapes=[pltpu.VMEM((ATTN_WIDTH, CHUNK), BF16),
                        pltpu.VMEM((2, ATTN_N_KV_HEADS, CHUNK, ATTN_HEAD_DIM), BF16),
                        pltpu.VMEM((2, KV_WIDTH, CHUNK), BF16),
                        pltpu.VMEM((2, tm // CHUNK, ATTN_WIDTH, CHUNK), BF16)],
        compiler_params=pltpu.CompilerParams(dimension_semantics=("arbitrary",), vmem_limit_bytes=VMEM_LIMIT),
        name="attn_mix",
    )(sinks, qkvt, pos_row, invf, yn, gates, x2, wso, wao, wmix, nw)


FFN_NCHUNK = 256
FFN_SUBTILE = 256
_GELU_A = float(-2.0 * np.sqrt(2.0 / np.pi) * np.log2(np.e))
_GELU_B = float(_GELU_A * 0.044715)


def _gelu_tanh(v):
    return v / (1.0 + jnp.exp2(v * (_GELU_A + _GELU_B * (v * v))))


def _ffn_kernel(x_ref, npre_ref, wup_ref, cw_ref, cb_ref, wdn_ref, npost_ref, o_ref, carry, act,
                *, tiles_per_seq):
    tm = x_ref.shape[0]
    ts = min(FFN_SUBTILE, tm)

    @pl.when(pl.program_id(0) % tiles_per_seq == 0)
    def _():
        carry[...] = jnp.zeros(carry.shape, F32)

    def up(h, c0):
        return (jnp.dot(h, wup_ref[:, c0:c0 + FFN_NCHUNK], preferred_element_type=F32),
                jnp.dot(h, wup_ref[:, FFN_D_FF + c0:FFN_D_FF + c0 + FFN_NCHUNK], preferred_element_type=F32))

    def conv_chunk(r, c0):
        cs = slice(c0, c0 + FFN_NCHUNK)
        out = _causal_conv_rows(r, carry[:, cs], cw_ref[:, cs], cb_ref[:, cs])
        carry[:, cs] = r[ts - CONV_HALO:ts, :]
        return out

    chunks = list(range(0, FFN_D_FF, FFN_NCHUNK))

    def up_stages(r0):
        x = x_ref[r0:r0 + ts, :]
        ms = jnp.mean(x * x, axis=-1, keepdims=True)
        h = (x * lax.rsqrt(ms + NORM_EPS) * npre_ref[...]).astype(BF16)
        nxt = up(h, chunks[0])
        for i, c0 in enumerate(chunks):
            cur = nxt
            if i + 1 < len(chunks):
                nxt = up(h, chunks[i + 1])
            gate = conv_chunk(cur[0], c0).astype(BF16)
            val = conv_chunk(cur[1], FFN_D_FF + c0).astype(BF16)
            act[r0:r0 + ts, c0:c0 + FFN_NCHUNK] = _gelu_tanh(gate) * val
            yield

    def down_stages(r0):
        a = act[r0:r0 + ts, :]
        ff = []
        for c in range(0, D_MODEL, FFN_NCHUNK):
            ff.append(jnp.dot(a, wdn_ref[:, c:c + FFN_NCHUNK], preferred_element_type=F32))
            yield
        ff = jnp.concatenate(ff, axis=1)
        ms2 = jnp.mean(ff * ff, axis=-1, keepdims=True)
        o_ref[r0:r0 + ts, :] = x_ref[r0:r0 + ts, :] + ff * lax.rsqrt(ms2 + NORM_EPS) * npost_ref[...]
        yield

    n_down = D_MODEL // FFN_NCHUNK + 1
    starts = list(range(0, tm, ts))
    for _ in up_stages(starts[0]):
        pass
    for prev_r0, r0 in zip(starts[:-1], starts[1:]):
        _deal(up_stages(r0), len(chunks), down_stages(prev_r0), n_down)
    for _ in down_stages(starts[-1]):
        pass


def _ffn(x1, npre, wup, cw, cb, wdn, npost, tm, batch):
    t = x1.shape[0]
    row = lambda w: pl.BlockSpec((tm, w), lambda i: (i, 0))
    return pl.pallas_call(
        functools.partial(_ffn_kernel, tiles_per_seq=t // batch // tm),
        grid=(t // tm,),
        in_specs=[row(D_MODEL), _resident((1, D_MODEL)), _resident((D_MODEL, 2 * FFN_D_FF)),
                  _resident((FFN_CONV_WIDTH, 2 * FFN_D_FF)), _resident((1, 2 * FFN_D_FF)),
                  _resident((FFN_D_FF, D_MODEL)), _resident((1, D_MODEL))],
        out_specs=row(D_MODEL),
        out_shape=jax.ShapeDtypeStruct((t, D_MODEL), F32),
        scratch_shapes=[pltpu.VMEM((CONV_HALO, 2 * FFN_D_FF), F32),
                        pltpu.VMEM((tm, FFN_D_FF), BF16)],
        compiler_params=pltpu.CompilerParams(dimension_semantics=("arbitrary",), vmem_limit_bytes=VMEM_LIMIT),
        name="ffn",
    )(x1, npre, wup, cw, cb, wdn, npost)


def _expansion_matrix():
    e = np.zeros((LANES, SSD_D_INNER), np.float32)
    ch = np.arange(SSD_D_INNER)
    for part in range(3):
        e[part * SSD_N_HEADS + ch // SSD_HEAD_DIM, ch] = 1.0
    return jnp.asarray(e, dtype=BF16)


def _rope_inv_freq():
    inv = ROPE_THETA ** (-jnp.arange(ROPE_HALF, dtype=F32) * 2.0 / ATTN_HEAD_DIM)
    return jnp.broadcast_to(inv[:, None], (ROPE_HALF, CHUNK))


def _layer(x2, pos_row, batch, norm_mix_pre_w, w_in, ssd_conv_w, ssd_conv_b, ssd_dt_bias, ssd_a_log, ssd_d,
           ssd_norm_w, ssd_w_out, attn_sinks, attn_w_out, w_mix_out, norm_mix_post_w, norm_ffn_pre_w,
           ffn_w_up, ffn_conv_w, ffn_conv_b, ffn_w_down, norm_ffn_post_w):
    o = np.cumsum((0, SSD_D_INNER, SSD_CONV_DIM, SSD_N_HEADS, ATTN_WIDTH, KV_WIDTH, KV_WIDTH, D_MODEL, D_MODEL))
    w_zx = w_in[:, o[0]:o[2]].astype(BF16)
    w_g = w_in[:, o[6]:o[8]].astype(BF16)
    w_dt = jnp.pad(w_in[:, o[2]:o[3]], ((0, 0), (0, DT_PAD - SSD_N_HEADS))).astype(BF16)
    w_qkv = w_in[:, o[3]:o[6]].astype(BF16)
    row = lambda v: v.reshape(1, -1).astype(F32)
    pad_heads = lambda v: jnp.pad(v.astype(F32), (0, DT_PAD - SSD_N_HEADS)).reshape(1, DT_PAD)

    gates, qkvt, yn = _proj_ssd(x2, row(norm_mix_pre_w), w_zx, w_g, w_dt, w_qkv, ssd_conv_w.astype(F32),
                                row(ssd_conv_b), pad_heads(ssd_dt_bias), pad_heads(ssd_a_log),
                                row(jnp.repeat(ssd_d, SSD_HEAD_DIM)), row(ssd_norm_w), _expansion_matrix(), TILE, batch)
    x1 = _attn_mix(attn_sinks.astype(F32), qkvt, pos_row, _rope_inv_freq(), yn, gates, x2, ssd_w_out.astype(BF16),
                   attn_w_out.astype(BF16), w_mix_out.astype(BF16), row(norm_mix_post_w), TILE, batch)
    return _ffn(x1, row(norm_ffn_pre_w), ffn_w_up.astype(BF16), ffn_conv_w.astype(F32), row(ffn_conv_b),
                ffn_w_down.astype(BF16), row(norm_ffn_post_w), TILE, batch)


def kernel(x, positions, norm_mix_pre_w, w_in, ssd_conv_w, ssd_conv_b, ssd_dt_bias, ssd_a_log, ssd_d, ssd_norm_w,
           ssd_w_out, attn_sinks, attn_w_out, w_mix_out, norm_mix_post_w, norm_ffn_pre_w, ffn_w_up, ffn_conv_w,
           ffn_conv_b, ffn_w_down, norm_ffn_post_w):
    batch, seq, d = x.shape
    assert d == D_MODEL and seq % TILE == 0
    x2 = x.reshape(batch * seq, d)
    pos_row = positions.reshape(1, batch * seq)
    for i in range(w_in.shape[0]):
        x2 = _layer(x2, pos_row, batch, norm_mix_pre_w[i], w_in[i], ssd_conv_w[i], ssd_conv_b[i], ssd_dt_bias[i],
                    ssd_a_log[i], ssd_d[i], ssd_norm_w[i], ssd_w_out[i], attn_sinks[i], attn_w_out[i],
                    w_mix_out[i], norm_mix_post_w[i], norm_ffn_pre_w[i], ffn_w_up[i], ffn_conv_w[i],
                    ffn_conv_b[i], ffn_w_down[i], norm_ffn_post_w[i])
    return x2.reshape(batch, seq, d)
```
